```python
import math
import jax, jax.numpy as jnp
from jax import lax
import numpy as np

D_MODEL = 2048
BATCH = 8
SEQ = 2048
DEPTH = 2

HEAD_DIM = 64
BLOCK = 128
A_Q_HEADS = 16
A_KV_HEADS = 2
A_GROUP = A_Q_HEADS // A_KV_HEADS
A_WINDOW = 128
B_HEADS = 8
B_PATTERNS = ((128, 1), (512, 4), (2048, 16))
C_HEADS = 16
D_HEADS = 16
D_Q_RANK = 512
D_KV_RANK = 256
D_NOPE = 64
D_ROPE = 32
D_V = 64
ROPE_BASE = 10000.0
D_FF = 4 * D_MODEL
LN_EPS = 1e-5
RMS_EPS = 1e-6
ALPHA = (2 * DEPTH) ** 0.25
BETA = (8 * DEPTH) ** -0.25
N_EVEN = (DEPTH + 1) // 2
N_ODD = DEPTH // 2
A_Q_W = A_Q_HEADS * HEAD_DIM
A_KV_W = A_KV_HEADS * HEAD_DIM
B_W = B_HEADS * HEAD_DIM
EVEN_IN = A_Q_W + 2 * A_KV_W + 3 * B_W * len(B_PATTERNS)
EVEN_OUT = A_Q_W + B_W
C_W = C_HEADS * HEAD_DIM
ODD_IN = 3 * C_W + D_Q_RANK + D_KV_RANK + D_ROPE
ODD_OUT = C_W + D_HEADS * D_V

kernel_name = 'hybrid_swa_dilated_stickbreak_mla_deepnorm'


def layer_norm(x, g, b):
    xf = x.astype(jnp.float32)
    mu = jnp.mean(xf, axis=-1, keepdims=True)
    xc = xf - mu
    var = jnp.mean(xc * xc, axis=-1, keepdims=True)
    y = xc * lax.rsqrt(var + LN_EPS) * g.astype(jnp.float32) + b.astype(jnp.float32)
    return y.astype(x.dtype)


def rms_norm(x, g):
    xf = x.astype(jnp.float32)
    y = xf * lax.rsqrt(jnp.mean(xf * xf, axis=-1, keepdims=True) + RMS_EPS) * g.astype(jnp.float32)
    return y.astype(x.dtype)


def alibi_slopes(n):
    return 2.0 ** (-8.0 * jnp.arange(1, n + 1, dtype=jnp.float32) / n)


def banded_attention(q, k, v, n_back, dist_scale, slopes, sinks=None):
    bsz, n, kh, g, dh = q.shape
    nb = -(-n // BLOCK)
    n_prev = -(-n_back // BLOCK)
    pad = nb * BLOCK - n
    qb = jnp.pad(q, ((0, 0), (0, pad), (0, 0), (0, 0), (0, 0))).reshape(bsz, nb, BLOCK, kh, g, dh)
    kv_pad = ((0, 0), (n_prev * BLOCK, pad), (0, 0), (0, 0))
    kp = jnp.pad(k, kv_pad).reshape(bsz, nb + n_prev, BLOCK, kh, dh)
    vp = jnp.pad(v, kv_pad).reshape(bsz, nb + n_prev, BLOCK, kh, dh)
    kb = jnp.concatenate([kp[:, p:p + nb] for p in range(n_prev + 1)], axis=2)
    vb = jnp.concatenate([vp[:, p:p + nb] for p in range(n_prev + 1)], axis=2)
    scores = jnp.einsum('bnqkgd,bnskd->bnkgqs', qb, kb).astype(jnp.float32) * (1.0 / math.sqrt(dh))
    n_keys = (n_prev + 1) * BLOCK
    qpos = jnp.arange(BLOCK)
    spos = jnp.arange(n_keys)
    rel = n_prev * BLOCK + qpos[:, None] - spos[None, :]
    key_idx = (jnp.arange(nb)[:, None] - n_prev) * BLOCK + spos[None, :]
    valid = (rel >= 0)[None] & (rel <= n_back)[None] & (key_idx >= 0)[:, None, :]
    bias = -slopes.astype(jnp.float32)[:, :, None, None] * (rel * dist_scale).astype(jnp.float32)
    scores = jnp.where(valid[None, :, None, None], scores + bias[None, None], -jnp.inf)
    m = jnp.max(scores, axis=-1)
    if sinks is not None:
        sink = sinks.astype(jnp.float32)[None, None, :, :, None]
        m = jnp.maximum(m, sink)
    p = jnp.exp(scores - m[..., None])
    denom = jnp.sum(p, axis=-1)
    if sinks is not None:
        denom = denom + jnp.exp(sink - m)
    out = jnp.einsum('bnkgqs,bnskd->bnqkgd', p.astype(v.dtype), vb).astype(jnp.float32)
    out = out / jnp.moveaxis(denom, -1, 2)[..., None]
    lse = jnp.moveaxis(m + jnp.log(denom), -1, 2)
    out = out.reshape(bsz, nb * BLOCK, kh, g, dh)[:, :n].astype(q.dtype)
    lse = lse.reshape(bsz, nb * BLOCK, kh, g)[:, :n]
    return out, lse


def to_strided(t, d):
    b, s = t.shape[:2]
    rest = t.shape[2:]
    t = jnp.moveaxis(t.reshape(b, s // d, d, *rest), 2, 1)
    return t.reshape(b * d, s // d, *rest)


def from_strided(t, b):
    bd, n = t.shape[:2]
    d = bd // b
    rest = t.shape[2:]
    t = jnp.moveaxis(t.reshape(b, d, n, *rest), 1, 2)
    return t.reshape(b, n * d, *rest)


def even_mixer(x, w_in, sinks, w_out):
    bsz, seq, _ = x.shape
    h = jnp.einsum('bsd,de->bse', x, w_in)
    qa, ka, va, hb = jnp.split(h, [A_Q_W, A_Q_W + A_KV_W, A_Q_W + 2 * A_KV_W], axis=-1)
    qa = qa.reshape(bsz, seq, A_KV_HEADS, A_GROUP, HEAD_DIM)
    ka = ka.reshape(bsz, seq, A_KV_HEADS, HEAD_DIM)
    va = va.reshape(bsz, seq, A_KV_HEADS, HEAD_DIM)
    oa, _ = banded_attention(qa, ka, va, A_WINDOW - 1, 1,
                             alibi_slopes(A_Q_HEADS).reshape(A_KV_HEADS, A_GROUP),
                             sinks.reshape(A_KV_HEADS, A_GROUP))
    slopes_b = alibi_slopes(B_HEADS).reshape(B_HEADS, 1)
    outs, lses = [], []
    for gi, (window, dil) in enumerate(B_PATTERNS):
        blk = hb[..., gi * 3 * B_W:(gi + 1) * 3 * B_W].reshape(bsz, seq, 3, B_HEADS, HEAD_DIM)
        qb, kb, vb = blk[:, :, 0], blk[:, :, 1], blk[:, :, 2]
        o, lse = banded_attention(to_strided(qb[:, :, :, None], dil), to_strided(kb, dil),
                                  to_strided(vb, dil), window // dil, dil, slopes_b)
        outs.append(from_strided(o[:, :, :, 0], bsz))
        lses.append(from_strided(lse[:, :, :, 0], bsz))
    mix = jax.nn.softmax(jnp.stack(lses), axis=0)
    ob = jnp.einsum('gbsh,gbshd->bshd', mix, jnp.stack(outs).astype(jnp.float32)).astype(x.dtype)
    y = jnp.concatenate([oa.reshape(bsz, seq, A_Q_W), ob.reshape(bsz, seq, B_W)], axis=-1)
    return jnp.einsum('bse,ed->bsd', y, w_out)


def stick_breaking_attention(q, k, v):
    bsz, seq, nh, dh = q.shape
    scale = 1.0 / math.sqrt(dh)
    outs = []
    for i in range(seq // BLOCK):
        lo, hi = i * BLOCK, (i + 1) * BLOCK
        z = jnp.einsum('bqhd,bshd->bhqs', q[:, lo:hi], k[:, :hi]).astype(jnp.float32) * scale
        strict = jnp.arange(hi)[None, :] < (lo + jnp.arange(BLOCK))[:, None]
        log_beta = jax.nn.log_sigmoid(z)
        log_keep = jnp.where(strict, jax.nn.log_sigmoid(-z), 0.0)
        after = lax.cumsum(log_keep, axis=3, reverse=True) - log_keep
        w = jnp.where(strict, jnp.exp(log_beta + after), 0.0)
        outs.append(jnp.einsum('bhqs,bshd->bqhd', w.astype(v.dtype), v[:, :hi]))
    return jnp.concatenate(outs, axis=1)


def apply_rope(x, cos, sin):
    half = x.shape[-1] // 2
    shape = (1, cos.shape[0]) + (1,) * (x.ndim - 3) + (half,)
    c = cos.reshape(shape).astype(x.dtype)
    s = sin.reshape(shape).astype(x.dtype)
    x1, x2 = x[..., :half], x[..., half:]
    return jnp.concatenate([x1 * c - x2 * s, x1 * s + x2 * c], axis=-1)


def mla_attention(q_nope, q_rope, k_nope, k_rope, v):
    seq = q_nope.shape[1]
    scale = 1.0 / math.sqrt(D_NOPE + D_ROPE)
    outs = []
    for i in range(seq // BLOCK):
        lo, hi = i * BLOCK, (i + 1) * BLOCK
        s = (jnp.einsum('bqhd,bshd->bhqs', q_nope[:, lo:hi], k_nope[:, :hi]).astype(jnp.float32)
             + jnp.einsum('bqhr,bsr->bhqs', q_rope[:, lo:hi], k_rope[:, :hi]).astype(jnp.float32)) * scale
        causal = jnp.arange(hi)[None, :] <= (lo + jnp.arange(BLOCK))[:, None]
        p = jax.nn.softmax(jnp.where(causal, s, -jnp.inf), axis=-1)
        outs.append(jnp.einsum('bhqs,bshd->bqhd', p.astype(v.dtype), v[:, :hi]))
    return jnp.concatenate(outs, axis=1)


def odd_mixer(x, w_in, q_norm_g, kv_norm_g, w_uq, w_ukv, w_out):
    bsz, seq, _ = x.shape
    h = jnp.einsum('bsd,de->bse', x, w_in)
    qc, kc, vc, cq, ckv, kr = jnp.split(
        h, [C_W, 2 * C_W, 3 * C_W, 3 * C_W + D_Q_RANK, 3 * C_W + D_Q_RANK + D_KV_RANK], axis=-1)
    oc = stick_breaking_attention(qc.reshape(bsz, seq, C_HEADS, HEAD_DIM),
                                  kc.reshape(bsz, seq, C_HEADS, HEAD_DIM),
                                  vc.reshape(bsz, seq, C_HEADS, HEAD_DIM))
    q = jnp.einsum('bsr,re->bse', rms_norm(cq, q_norm_g), w_uq).reshape(bsz, seq, D_HEADS, D_NOPE + D_ROPE)
    kv = jnp.einsum('bsr,re->bse', rms_norm(ckv, kv_norm_g), w_ukv).reshape(bsz, seq, D_HEADS, D_NOPE + D_V)
    q_nope, q_rope = q[..., :D_NOPE], q[..., D_NOPE:]
    k_nope, v = kv[..., :D_NOPE], kv[..., D_NOPE:]
    inv_freq = ROPE_BASE ** (-jnp.arange(0, D_ROPE, 2, dtype=jnp.float32) / D_ROPE)
    ang = jnp.arange(seq, dtype=jnp.float32)[:, None] * inv_freq[None, :]
    cos, sin = jnp.cos(ang), jnp.sin(ang)
    od = mla_attention(q_nope, apply_rope(q_rope, cos, sin), k_nope, apply_rope(kr, cos, sin), v)
    y = jnp.concatenate([oc.reshape(bsz, seq, C_W), od.reshape(bsz, seq, D_HEADS * D_V)], axis=-1)
    return jnp.einsum('bse,ed->bsd', y, w_out)


def sqrelu_mlp(x, w1, w2):
    hid = jax.nn.relu(jnp.einsum('bsd,df->bsf', x, w1))
    return jnp.einsum('bsf,fd->bsd', hid * hid, w2)


def setup_inputs(seed: int = 0) -> dict:
    key = jax.random.key(seed)
    ks = jax.random.split(key, 16)
    f32 = jnp.float32

    def normal(k, shape, scale):
        return jax.random.normal(k, shape, f32) * scale

    x = normal(ks[0], (BATCH, SEQ, D_MODEL), 1.0)
    b_group_col = jnp.concatenate([jnp.ones(2 * B_W, f32), jnp.full((B_W,), BETA, f32)])
    even_col = jnp.concatenate([jnp.ones(A_Q_W + A_KV_W, f32), jnp.full((A_KV_W,), BETA, f32)]
                               + [b_group_col] * len(B_PATTERNS))
    even_w_in = normal(ks[1], (N_EVEN, D_MODEL, EVEN_IN), D_MODEL ** -0.5) * even_col
    even_sinks = 1.0 + normal(ks[2], (N_EVEN, A_Q_HEADS), 1.0)
    even_w_out = normal(ks[3], (N_EVEN, EVEN_OUT, D_MODEL), BETA * EVEN_OUT ** -0.5)
    odd_col = jnp.concatenate([jnp.ones(2 * C_W, f32), jnp.full((C_W,), BETA, f32),
                               jnp.ones(D_Q_RANK + D_KV_RANK + D_ROPE, f32)])
    odd_w_in = normal(ks[4], (N_ODD, D_MODEL, ODD_IN), D_MODEL ** -0.5) * odd_col
    odd_q_norm_g = 1.0 + normal(ks[5], (N_ODD, D_Q_RANK), 0.02)
    odd_kv_norm_g = 1.0 + normal(ks[6], (N_ODD, D_KV_RANK), 0.02)
    odd_w_uq = normal(ks[7], (N_ODD, D_Q_RANK, D_HEADS * (D_NOPE + D_ROPE)), D_Q_RANK ** -0.5)
    ukv_col = jnp.tile(jnp.concatenate([jnp.ones(D_NOPE, f32), jnp.full((D_V,), BETA, f32)]), D_HEADS)
    odd_w_ukv = normal(ks[8], (N_ODD, D_KV_RANK, D_HEADS * (D_NOPE + D_V)), D_KV_RANK ** -0.5) * ukv_col
    odd_w_out = normal(ks[9], (N_ODD, ODD_OUT, D_MODEL), BETA * ODD_OUT ** -0.5)
    ln1_g = 1.0 + normal(ks[10], (DEPTH, D_MODEL), 0.02)
    ln1_b = normal(ks[11], (DEPTH, D_MODEL), 0.02)
    mlp_w1 = normal(ks[12], (DEPTH, D_MODEL, D_FF), D_MODEL ** -0.5)
    mlp_w2 = normal(ks[13], (DEPTH, D_FF, D_MODEL), BETA * D_FF ** -0.5)
    ln2_g = 1.0 + normal(ks[14], (DEPTH, D_MODEL), 0.02)
    ln2_b = normal(ks[15], (DEPTH, D_MODEL), 0.02)
    return {'x': x, 'even_w_in': even_w_in, 'even_sinks': even_sinks, 'even_w_out': even_w_out,
            'odd_w_in': odd_w_in, 'odd_q_norm_g': odd_q_norm_g, 'odd_kv_norm_g': odd_kv_norm_g,
            'odd_w_uq': odd_w_uq, 'odd_w_ukv': odd_w_ukv, 'odd_w_out': odd_w_out,
            'ln1_g': ln1_g, 'ln1_b': ln1_b, 'mlp_w1': mlp_w1, 'mlp_w2': mlp_w2,
            'ln2_g': ln2_g, 'ln2_b': ln2_b}


def reference(x, even_w_in, even_sinks, even_w_out, odd_w_in, odd_q_norm_g, odd_kv_norm_g,
              odd_w_uq, odd_w_ukv, odd_w_out, ln1_g, ln1_b, mlp_w1, mlp_w2, ln2_g, ln2_b):
    for layer in range(DEPTH):
        j = layer // 2
        if layer % 2 == 0:
            mixed = even_mixer(x, even_w_in[j], even_sinks[j], even_w_out[j])
        else:
            mixed = odd_mixer(x, odd_w_in[j], odd_q_norm_g[j], odd_kv_norm_g[j],
                              odd_w_uq[j], odd_w_ukv[j], odd_w_out[j])
        x = layer_norm(ALPHA * x + mixed, ln1_g[layer], ln1_b[layer])
        x = layer_norm(ALPHA * x + sqrelu_mlp(x, mlp_w1[layer], mlp_w2[layer]), ln2_g[layer], ln2_b[layer])
    return x
```

```python
import functools
import math

import jax
import jax.numpy as jnp
from jax import lax
from jax.experimental import pallas as pl
from jax.experimental.pallas import tpu as pltpu

LANE = 128
V7X_VMEM_BYTES = 64 * 1024 * 1024
VMEM_LIMIT = V7X_VMEM_BYTES - 8 * 1024 * 1024

HEAD_DIM = 64
A_Q_HEADS = 16
A_KV_HEADS = 2
A_WINDOW = 128
B_HEADS = 8
B_PATTERNS = ((128, 1), (512, 4), (2048, 16))
C_HEADS = 16
D_HEADS = 16
D_Q_RANK = 512
D_KV_RANK = 256
D_NOPE = 64
D_ROPE = 32
D_V = 64
ROPE_BASE = 10000.0
LN_EPS = 1e-5
RMS_EPS = 1e-6
BLOCK = 128
NEG = -1e30

A_Q_W = A_Q_HEADS * HEAD_DIM
A_KV_W = A_KV_HEADS * HEAD_DIM
B_W = B_HEADS * HEAD_DIM
C_W = C_HEADS * HEAD_DIM

BF16 = jnp.bfloat16
F32 = jnp.float32


def _cparams(sem):
    return pltpu.CompilerParams(dimension_semantics=sem, vmem_limit_bytes=VMEM_LIMIT)


def _dot(a, b):
    return jnp.dot(a, b, preferred_element_type=F32)


def _dot_nt(a, b):
    return lax.dot_general(a, b, (((1,), (1,)), ((), ())), preferred_element_type=F32)


def _layer_norm(r, g, b):
    mu = jnp.mean(r, axis=-1, keepdims=True)
    rc = r - mu
    var = jnp.mean(rc * rc, axis=-1, keepdims=True)
    return rc * lax.rsqrt(var + LN_EPS) * g + b


def _proj_kernel(x_ref, w_ref, o_ref):
    xb = x_ref[0].astype(BF16)
    acc = _dot(xb, w_ref[...])
    for c in range(o_ref.shape[1]):
        o_ref[0, c] = acc[:, c * LANE:(c + 1) * LANE].astype(o_ref.dtype)


def _proj(x, w, *, tm, n_split):
    bsz, seq, d = x.shape
    tiles = w.shape[1] // LANE
    tps = tiles // n_split
    return pl.pallas_call(
        _proj_kernel,
        grid=(n_split, bsz, seq // tm),
        in_specs=[pl.BlockSpec((1, tm, d), lambda n, b, s: (b, s, 0)),
                  pl.BlockSpec((d, tps * LANE), lambda n, b, s: (0, n))],
        out_specs=pl.BlockSpec((1, tps, tm, LANE), lambda n, b, s: (b, n, s, 0)),
        out_shape=jax.ShapeDtypeStruct((bsz, tiles, seq, LANE), BF16),
        compiler_params=_cparams(("arbitrary", "arbitrary", "arbitrary")),
        name="proj",
    )(x, w)


def _split_heads(q2, lo):
    zero = jnp.zeros_like(q2)
    return jnp.concatenate([jnp.where(lo, q2, zero), jnp.where(lo, zero, q2)], axis=0)


def _merge_heads(o, lo, t):
    return jnp.where(lo, o[:t], o[t:])


def _softmax_step(s, v, m_ref, l_ref, acc_ref):
    m_prev = m_ref[...]
    m_new = jnp.maximum(m_prev, jnp.max(s, axis=-1, keepdims=True))
    alpha = jnp.exp(m_prev - m_new)
    p = jnp.exp(s - m_new)
    l_ref[...] = alpha * l_ref[...] + jnp.sum(p, axis=-1, keepdims=True)
    acc_ref[...] = alpha * acc_ref[...] + _dot(p.astype(v.dtype), v)
    m_ref[...] = m_new


def _tile_geometry(t):
    row = lax.broadcasted_iota(jnp.int32, (2 * t, t), 0)
    col = lax.broadcasted_iota(jnp.int32, (2 * t, t), 1)
    qpos = jnp.where(row >= t, row - t, row)
    return qpos - col


def _attn_a_kernel(sinks_ref, q_ref, k_ref, v_ref, o_ref, m_ref, l_ref, acc_ref, *, t):
    i = pl.program_id(1)
    n_tiles = q_ref.shape[1]
    heads_per_kv = A_Q_HEADS // A_KV_HEADS
    lo = lax.broadcasted_iota(jnp.int32, (1, LANE), 1) < HEAD_DIM
    rel0 = _tile_geometry(t)
    rel0f = rel0.astype(F32)
    causal = rel0 >= 0
    upper = rel0 < 0
    rowc = lax.broadcasted_iota(jnp.int32, (2 * t, 1), 0)
    hi_half = rowc >= t

    def per_tile(hp, carry):
        q2 = q_ref[0, hp] * jnp.asarray(1.0 / math.sqrt(HEAD_DIM), BF16)
        q = _split_heads(q2, lo)
        head = (2 * hp + hi_half.astype(jnp.int32)).astype(F32)
        slope = jnp.exp2(-(head + 1.0) * (8.0 / A_Q_HEADS))
        sink = jnp.where(hi_half, sinks_ref[2 * hp + 1], sinks_ref[2 * hp])
        kh = (2 * hp) // heads_per_kv
        m_ref[...] = sink
        l_ref[...] = jnp.ones_like(l_ref)
        acc_ref[...] = jnp.zeros_like(acc_ref)

        cur = pl.ds(pl.multiple_of(i * t, t), t)
        s = _dot_nt(q, k_ref[0, kh, cur, :]) - slope * rel0f
        _softmax_step(jnp.where(causal, s, NEG), v_ref[0, kh, cur, :], m_ref, l_ref, acc_ref)

        @pl.when(i > 0)
        def _():
            prev = pl.ds(pl.multiple_of((i - 1) * t, t), t)
            s = _dot_nt(q, k_ref[0, kh, prev, :]) - slope * (rel0f + float(t))
            _softmax_step(jnp.where(upper, s, NEG), v_ref[0, kh, prev, :], m_ref, l_ref, acc_ref)

        o = acc_ref[...] / l_ref[...]
        o_ref[0, hp] = _merge_heads(o, lo, t).astype(o_ref.dtype)
        return carry

    lax.fori_loop(0, n_tiles, per_tile, 0)


def _attn_a(h, sinks, *, q_tile0, k_tile0, v_tile0):
    bsz, _, seq, _ = h.shape
    t = A_WINDOW
    nq = A_Q_HEADS // 2
    return pl.pallas_call(
        functools.partial(_attn_a_kernel, t=t),
        grid=(bsz, seq // t),
        in_specs=[pl.BlockSpec(memory_space=pltpu.SMEM),
                  pl.BlockSpec((1, nq, t, LANE), lambda b, i: (b, q_tile0 // nq, i, 0)),
                  pl.BlockSpec((1, A_KV_HEADS, seq, LANE), lambda b, i: (b, k_tile0 // A_KV_HEADS, 0, 0)),
                  pl.BlockSpec((1, A_KV_HEADS, seq, LANE), lambda b, i: (b, v_tile0 // A_KV_HEADS, 0, 0))],
        out_specs=pl.BlockSpec((1, nq, t, LANE), lambda b, i: (b, 0, i, 0)),
        out_shape=jax.ShapeDtypeStruct((bsz, nq, seq, LANE), BF16),
        scratch_shapes=[pltpu.VMEM((2 * t, 1), F32), pltpu.VMEM((2 * t, 1), F32),
                        pltpu.VMEM((2 * t, LANE), F32)],
        compiler_params=_cparams(("arbitrary", "arbitrary")),
        name="attn_a",
    )(sinks, h, h, h)


def _attn_b_kernel(q0_ref, k0_ref, v0_ref, q1_ref, k1_ref, v1_ref, q2_ref, k2_ref, v2_ref,
                   o_ref, m_ref, l_ref, acc_ref, *, t):
    i = pl.program_id(1)
    n_tiles = q0_ref.shape[1]
    lo = lax.broadcasted_iota(jnp.int32, (1, LANE), 1) < HEAD_DIM
    rel0 = _tile_geometry(t)
    rel0f = rel0.astype(F32)
    rowc = lax.broadcasted_iota(jnp.int32, (2 * t, 1), 0)
    hi_half = rowc >= t
    groups = ((q0_ref, k0_ref, v0_ref), (q1_ref, k1_ref, v1_ref), (q2_ref, k2_ref, v2_ref))

    def per_tile(hp, carry):
        head = (2 * hp + hi_half.astype(jnp.int32)).astype(F32)
        slope = jnp.exp2(-(head + 1.0) * (8.0 / B_HEADS))
        nbias = -slope * rel0f
        m_ref[...] = jnp.full_like(m_ref, NEG)
        l_ref[...] = jnp.zeros_like(l_ref)
        acc_ref[...] = jnp.zeros_like(acc_ref)

        for (q_ref, k_ref, v_ref), (window, dil) in zip(groups, B_PATTERNS):
            q2 = q_ref[0, hp] * jnp.asarray(1.0 / math.sqrt(HEAD_DIM), BF16)
            q = _split_heads(q2, lo)
            on_grid = (rel0 & (dil - 1)) == 0
            back = window // t

            def step(kb, mask, q=q, k_ref=k_ref, v_ref=v_ref):
                rows = pl.ds(pl.multiple_of(kb * t, t), t)
                off = ((i - kb) * t).astype(F32)
                s = _dot_nt(q, k_ref[0, hp, rows, :]) + nbias - slope * off
                _softmax_step(jnp.where(mask, s, NEG), v_ref[0, hp, rows, :], m_ref, l_ref, acc_ref)

            step(i, on_grid & (rel0 >= 0))
            if back >= seq_tiles(k_ref, t):
                def body(kb, c, step=step, on_grid=on_grid):
                    step(kb, on_grid)
                    return c
                lax.fori_loop(0, i, body, 0)
            else:
                for j in range(1, back + 1):
                    mask = on_grid if j < back else on_grid & (rel0 <= 0)

                    @pl.when(i >= j)
                    def _(j=j, mask=mask, step=step):
                        step(i - j, mask)

        o = acc_ref[...] / l_ref[...]
        o_ref[0, hp] = _merge_heads(o, lo, t).astype(o_ref.dtype)
        return carry

    lax.fori_loop(0, n_tiles, per_tile, 0)


def seq_tiles(k_ref, t):
    return k_ref.shape[2] // t


def _attn_b(h, *, tile0):
    bsz, _, seq, _ = h.shape
    t = BLOCK
    n = B_HEADS // 2
    in_specs = []
    for g in range(len(B_PATTERNS)):
        base = tile0 + 3 * n * g
        in_specs.append(pl.BlockSpec((1, n, t, LANE), lambda b, i, base=base: (b, base // n, i, 0)))
        in_specs.append(pl.BlockSpec((1, n, seq, LANE), lambda b, i, base=base: (b, base // n + 1, 0, 0)))
        in_specs.append(pl.BlockSpec((1, n, seq, LANE), lambda b, i, base=base: (b, base // n + 2, 0, 0)))
    return pl.pallas_call(
        functools.partial(_attn_b_kernel, t=t),
        grid=(bsz, seq // t),
        in_specs=in_specs,
        out_specs=pl.BlockSpec((1, n, t, LANE), lambda b, i: (b, 0, i, 0)),
        out_shape=jax.ShapeDtypeStruct((bsz, n, seq, LANE), BF16),
        scratch_shapes=[pltpu.VMEM((2 * t, 1), F32), pltpu.VMEM((2 * t, 1), F32),
                        pltpu.VMEM((2 * t, LANE), F32)],
        compiler_params=_cparams(("arbitrary", "arbitrary")),
        name="attn_b",
    )(*([h] * 9))


def _attn_c_kernel(q_ref, k_ref, v_ref, o_ref, c_ref, acc_ref, *, t):
    i = pl.program_id(1)
    n_tiles = q_ref.shape[1]
    lo = lax.broadcasted_iota(jnp.int32, (1, LANE), 1) < HEAD_DIM
    strict = _tile_geometry(t) > 0
    jj = lax.broadcasted_iota(jnp.int32, (t, t), 0)
    ss = lax.broadcasted_iota(jnp.int32, (t, t), 1)
    tri = jnp.where(jj > ss, 1.0, 0.0).astype(BF16)

    def per_tile(hp, carry):
        q2 = q_ref[0, hp] * jnp.asarray(1.0 / math.sqrt(HEAD_DIM), BF16)
        q = _split_heads(q2, lo)
        c_ref[...] = jnp.zeros_like(c_ref)
        acc_ref[...] = jnp.zeros_like(acc_ref)

        def step(kb, diagonal):
            rows = pl.ds(pl.multiple_of(kb * t, t), t)
            z = _dot_nt(q, k_ref[0, hp, rows, :])
            log_beta = jnp.minimum(z, 0.0) - jnp.log1p(jnp.exp(-jnp.abs(z)))
            log_keep = log_beta - z
            if diagonal:
                log_keep = jnp.where(strict, log_keep, 0.0)
            keep_hi = log_keep.astype(BF16)
            keep_lo = (log_keep - keep_hi.astype(F32)).astype(BF16)
            after = _dot(keep_hi, tri) + _dot(keep_lo, tri) + c_ref[...]
            w = jnp.exp(log_beta + after)
            if diagonal:
                w = jnp.where(strict, w, 0.0)
            acc_ref[...] += _dot(w.astype(BF16), v_ref[0, hp, rows, :])
            c_ref[...] += jnp.sum(log_keep, axis=-1, keepdims=True)

        step(i, True)

        def body(j, c):
            step(i - 1 - j, False)
            return c

        lax.fori_loop(0, i, body, 0)
        o_ref[0, hp] = _merge_heads(acc_ref[...], lo, t).astype(o_ref.dtype)
        return carry

    lax.fori_loop(0, n_tiles, per_tile, 0)


def _attn_c(h, *, t):
    bsz, _, seq, _ = h.shape
    n = C_HEADS // 2
    return pl.pallas_call(
        functools.partial(_attn_c_kernel, t=t),
        grid=(bsz, seq // t),
        in_specs=[pl.BlockSpec((1, n, t, LANE), lambda b, i: (b, 0, i, 0)),
                  pl.BlockSpec((1, n, seq, LANE), lambda b, i: (b, 1, 0, 0)),
                  pl.BlockSpec((1, n, seq, LANE), lambda b, i: (b, 2, 0, 0))],
        out_specs=pl.BlockSpec((1, n, t, LANE), lambda b, i: (b, 0, i, 0)),
        out_shape=jax.ShapeDtypeStruct((bsz, n, seq, LANE), BF16),
        scratch_shapes=[pltpu.VMEM((2 * t, 1), F32), pltpu.VMEM((2 * t, LANE), F32)],
        compiler_params=_cparams(("arbitrary", "arbitrary")),
        name="attn_c",
    )(h, h, h)


def _rope(x, cos_t, sin_t, first_half):
    partner = jnp.where(first_half, pltpu.roll(x, LANE - D_ROPE // 2, 1), pltpu.roll(x, D_ROPE // 2, 1))
    return x * cos_t + partner * sin_t


def _mla_proj_kernel(x_ref, wl_ref, gq_ref, gkv_ref, wuq_ref, wukv_ref, cos_ref, sin_ref,
                     qn_ref, qr_ref, kn_ref, vd_ref, kr_ref):
    xb = x_ref[0].astype(BF16)
    lat = _dot(xb, wl_ref[...])
    cq = lat[:, :D_Q_RANK]
    ckv = lat[:, D_Q_RANK:D_Q_RANK + D_KV_RANK]
    kr = lat[:, D_Q_RANK + D_KV_RANK:]
    cqn = cq * lax.rsqrt(jnp.mean(cq * cq, axis=-1, keepdims=True) + RMS_EPS) * gq_ref[...]
    ckvn = ckv * lax.rsqrt(jnp.mean(ckv * ckv, axis=-1, keepdims=True) + RMS_EPS) * gkv_ref[...]
    q = _dot(cqn.astype(BF16), wuq_ref[...])
    kv = _dot(ckvn.astype(BF16), wukv_ref[...])
    cos_t = cos_ref[...]
    sin_t = sin_ref[...]
    first_half = (lax.broadcasted_iota(jnp.int32, (1, LANE), 1) % D_ROPE) < D_ROPE // 2
    n_nope = qn_ref.shape[1]
    for c in range(n_nope):
        qn_ref[0, c] = q[:, c * LANE:(c + 1) * LANE].astype(BF16)
        kn_ref[0, c] = kv[:, c * LANE:(c + 1) * LANE].astype(BF16)
        vd_ref[0, c] = kv[:, (n_nope + c) * LANE:(n_nope + c + 1) * LANE].astype(BF16)
    for c in range(qr_ref.shape[1]):
        qr = q[:, (n_nope + c) * LANE:(n_nope + c + 1) * LANE]
        qr_ref[0, c] = _rope(qr, cos_t, sin_t, first_half).astype(BF16)
    kr_ref[0] = _rope(kr, cos_t, sin_t, first_half).astype(BF16)


def _mla_proj(x, w_lat, gq, gkv, w_uq, w_ukv, cos_t, sin_t, *, tm):
    bsz, seq, d = x.shape
    n_nope = D_HEADS * D_NOPE // LANE
    n_rope = D_HEADS * D_ROPE // LANE
    full = lambda a: pl.BlockSpec(a.shape, lambda b, s: (0,) * a.ndim)
    tile_out = lambda n: pl.BlockSpec((1, n, tm, LANE), lambda b, s: (b, 0, s, 0))
    tile_shape = lambda n: jax.ShapeDtypeStruct((bsz, n, seq, LANE), BF16)
    return pl.pallas_call(
        _mla_proj_kernel,
        grid=(bsz, seq // tm),
        in_specs=[pl.BlockSpec((1, tm, d), lambda b, s: (b, s, 0)),
                  full(w_lat), full(gq), full(gkv), full(w_uq), full(w_ukv),
                  pl.BlockSpec((tm, LANE), lambda b, s: (s, 0)),
                  pl.BlockSpec((tm, LANE), lambda b, s: (s, 0))],
        out_specs=[tile_out(n_nope), tile_out(n_rope), tile_out(n_nope), tile_out(n_nope),
                   pl.BlockSpec((1, tm, LANE), lambda b, s: (b, s, 0))],
        out_shape=[tile_shape(n_nope), tile_shape(n_rope), tile_shape(n_nope), tile_shape(n_nope),
                   jax.ShapeDtypeStruct((bsz, seq, LANE), BF16)],
        compiler_params=_cparams(("arbitrary", "arbitrary")),
        name="mla_proj",
    )(x, w_lat, gq, gkv, w_uq, w_ukv, cos_t, sin_t)


def _attn_d_kernel(qn_ref, qr_ref, kn_ref, kr_ref, vd_ref, o_ref, m_ref, l_ref, acc_ref, *, t):
    i = pl.program_id(1)
    n_tiles = qn_ref.shape[1]
    lane = lax.broadcasted_iota(jnp.int32, (1, LANE), 1)
    lo = lane < D_NOPE
    lane_q = lane // D_ROPE
    causal = _tile_geometry(t) >= 0
    scale = 1.0 / math.sqrt(D_NOPE + D_ROPE)

    def per_tile(hp, carry):
        qn = _split_heads(qn_ref[0, hp], lo)
        qr2 = qr_ref[0, hp // 2]
        quarter = (hp % 2) * 2
        zero = jnp.zeros_like(qr2)
        qr = jnp.concatenate([jnp.where(lane_q == quarter, qr2, zero),
                              jnp.where(lane_q == quarter + 1, qr2, zero)], axis=0)
        q = jnp.concatenate([qn, qr], axis=1)
        m_ref[...] = jnp.full_like(m_ref, NEG)
        l_ref[...] = jnp.zeros_like(l_ref)
        acc_ref[...] = jnp.zeros_like(acc_ref)

        def step(kb, diagonal):
            rows = pl.ds(pl.multiple_of(kb * t, t), t)
            k = jnp.concatenate([kn_ref[0, hp, rows, :], kr_ref[0, rows, :]], axis=1)
            s = _dot_nt(q, k) * scale
            if diagonal:
                s = jnp.where(causal, s, NEG)
            _softmax_step(s, vd_ref[0, hp, rows, :], m_ref, l_ref, acc_ref)

        step(i, True)

        def body(kb, c):
            step(kb, False)
            return c

        lax.fori_loop(0, i, body, 0)
        o = acc_ref[...] / l_ref[...]
        o_ref[0, hp] = _merge_heads(o, lo, t).astype(o_ref.dtype)
        return carry

    lax.fori_loop(0, n_tiles, per_tile, 0)


def _attn_d(qn, qr, kn, kr, vd, *, t):
    bsz, n, seq, _ = qn.shape
    nr = qr.shape[1]
    return pl.pallas_call(
        functools.partial(_attn_d_kernel, t=t),
        grid=(bsz, seq // t),
        in_specs=[pl.BlockSpec((1, n, t, LANE), lambda b, i: (b, 0, i, 0)),
                  pl.BlockSpec((1, nr, t, LANE), lambda b, i: (b, 0, i, 0)),
                  pl.BlockSpec((1, n, seq, LANE), lambda b, i: (b, 0, 0, 0)),
                  pl.BlockSpec((1, seq, LANE), lambda b, i: (b, 0, 0)),
                  pl.BlockSpec((1, n, seq, LANE), lambda b, i: (b, 0, 0, 0))],
        out_specs=pl.BlockSpec((1, n, t, LANE), lambda b, i: (b, 0, i, 0)),
        out_shape=jax.ShapeDtypeStruct((bsz, n, seq, LANE), BF16),
        scratch_shapes=[pltpu.VMEM((2 * t, 1), F32), pltpu.VMEM((2 * t, 1), F32),
                        pltpu.VMEM((2 * t, LANE), F32)],
        compiler_params=_cparams(("arbitrary", "arbitrary")),
        name="attn_d",
    )(qn, qr, kn, kr, vd)


def _out_ln_kernel(x_ref, ya_ref, yb_ref, wa_ref, wb_ref, g_ref, b_ref, o_ref, *, alpha):
    ya = jnp.concatenate([ya_ref[0, c] for c in range(ya_ref.shape[1])], axis=1)
    yb = jnp.concatenate([yb_ref[0, c] for c in range(yb_ref.shape[1])], axis=1)
    mixed = _dot(ya, wa_ref[...]) + _dot(yb, wb_ref[...])
    o_ref[0] = _layer_norm(alpha * x_ref[0] + mixed, g_ref[...], b_ref[...])


def _out_ln(x, ya, yb, wa, wb, g, b, *, alpha, tm):
    bsz, seq, d = x.shape
    full = lambda a: pl.BlockSpec(a.shape, lambda bb, s: (0,) * a.ndim)
    return pl.pallas_call(
        functools.partial(_out_ln_kernel, alpha=alpha),
        grid=(bsz, seq // tm),
        in_specs=[pl.BlockSpec((1, tm, d), lambda bb, s: (bb, s, 0)),
                  pl.BlockSpec((1, ya.shape[1], tm, LANE), lambda bb, s: (bb, 0, s, 0)),
                  pl.BlockSpec((1, yb.shape[1], tm, LANE), lambda bb, s: (bb, 0, s, 0)),
                  full(wa), full(wb), full(g), full(b)],
        out_specs=pl.BlockSpec((1, tm, d), lambda bb, s: (bb, s, 0)),
        out_shape=jax.ShapeDtypeStruct((bsz, seq, d), F32),
        compiler_params=_cparams(("arbitrary", "arbitrary")),
        name="out_ln",
    )(x, ya, yb, wa, wb, g, b)


def _mlp_ln_kernel(x_ref, w1_ref, w2_ref, g_ref, b_ref, o_ref, xb_ref, acc_ref, *, alpha):
    f = pl.program_id(1)

    @pl.when(f == 0)
    def _():
        xb_ref[...] = x_ref[...].astype(BF16)
        acc_ref[...] = jnp.zeros_like(acc_ref)

    hid = jnp.maximum(_dot(xb_ref[...], w1_ref[...]), 0.0)
    acc_ref[...] += _dot((hid * hid).astype(BF16), w2_ref[...])

    @pl.when(f == pl.num_programs(1) - 1)
    def _():
        o_ref[...] = _layer_norm(alpha * x_ref[...] + acc_ref[...], g_ref[...], b_ref[...])


def _mlp_ln(x, w1, w2, g, b, *, alpha, tm, tf):
    m, d = x.shape
    ff = w1.shape[1]
    return pl.pallas_call(
        functools.partial(_mlp_ln_kernel, alpha=alpha),
        grid=(m // tm, ff // tf),
        in_specs=[pl.BlockSpec((tm, d), lambda i, f: (i, 0)),
                  pl.BlockSpec((d, tf), lambda i, f: (0, f)),
                  pl.BlockSpec((tf, d), lambda i, f: (f, 0)),
                  pl.BlockSpec((1, d), lambda i, f: (0, 0)),
                  pl.BlockSpec((1, d), lambda i, f: (0, 0))],
        out_specs=pl.BlockSpec((tm, d), lambda i, f: (i, 0)),
        out_shape=jax.ShapeDtypeStruct((m, d), F32),
        scratch_shapes=[pltpu.VMEM((tm, d), BF16), pltpu.VMEM((tm, d), F32)],
        compiler_params=_cparams(("arbitrary", "arbitrary")),
        name="mlp_ln",
    )(x, w1, w2, g, b)


def _even_w_in_layout(w):
    qa = w[:, :A_Q_W]
    ka = w[:, A_Q_W:A_Q_W + A_KV_W]
    va = w[:, A_Q_W + A_KV_W:A_Q_W + 2 * A_KV_W]
    hb = w[:, A_Q_W + 2 * A_KV_W:]
    dup = lambda kv: [kv[:, h * HEAD_DIM:(h + 1) * HEAD_DIM] for h in range(A_KV_HEADS) for _ in range(2)]
    return jnp.concatenate([qa, hb] + dup(ka) + dup(va), axis=1).astype(BF16)


def _rope_tables(seq):
    inv_freq = ROPE_BASE ** (-jnp.arange(0, D_ROPE, 2, dtype=F32) / D_ROPE)
    ang = jnp.arange(seq, dtype=F32)[:, None] * inv_freq[None, :]
    cos, sin = jnp.cos(ang), jnp.sin(ang)
    reps = LANE // D_ROPE
    cos_t = jnp.tile(jnp.concatenate([cos, cos], axis=1), (1, reps))
    sin_t = jnp.tile(jnp.concatenate([-sin, sin], axis=1), (1, reps))
    return cos_t, sin_t


def _even_layer(x, w_in, sinks, w_out, g, b, alpha):
    h = _proj(x, _even_w_in_layout(w_in), tm=512, n_split=2)
    n_qa = A_Q_W // LANE
    n_b = 3 * len(B_PATTERNS) * B_W // LANE
    oa = _attn_a(h, sinks, q_tile0=0, k_tile0=n_qa + n_b, v_tile0=n_qa + n_b + A_KV_HEADS)
    ob = _attn_b(h, tile0=n_qa)
    w_out = w_out.astype(BF16)
    return _out_ln(x, oa, ob, w_out[:A_Q_W], w_out[A_Q_W:], g, b, alpha=alpha, tm=512)


def _odd_layer(x, w_in, gq, gkv, w_uq, w_ukv, w_out, g, b, alpha):
    seq = x.shape[1]
    hc = _proj(x, w_in[:, :3 * C_W].astype(BF16), tm=512, n_split=1)
    oc = _attn_c(hc, t=256)
    w_lat = w_in[:, 3 * C_W:]
    kr_w = w_lat[:, D_Q_RANK + D_KV_RANK:]
    w_lat = jnp.concatenate([w_lat[:, :D_Q_RANK + D_KV_RANK]] + [kr_w] * (LANE // D_ROPE), axis=1).astype(BF16)
    uq = w_uq.reshape(D_Q_RANK, D_HEADS, D_NOPE + D_ROPE)
    uq = jnp.concatenate([uq[:, :, :D_NOPE].reshape(D_Q_RANK, -1), uq[:, :, D_NOPE:].reshape(D_Q_RANK, -1)], axis=1)
    ukv = w_ukv.reshape(D_KV_RANK, D_HEADS, D_NOPE + D_V)
    ukv = jnp.concatenate([ukv[:, :, :D_NOPE].reshape(D_KV_RANK, -1), ukv[:, :, D_NOPE:].reshape(D_KV_RANK, -1)], axis=1)
    cos_t, sin_t = _rope_tables(seq)
    qn, qr, kn, vd, kr = _mla_proj(x, w_lat, gq.reshape(1, -1), gkv.reshape(1, -1),
                                   uq.astype(BF16), ukv.astype(BF16), cos_t, sin_t, tm=512)
    od = _attn_d(qn, qr, kn, kr, vd, t=256)
    w_out = w_out.astype(BF16)
    return _out_ln(x, oc, od, w_out[:C_W], w_out[C_W:], g, b, alpha=alpha, tm=512)


def kernel(x, even_w_in, even_sinks, even_w_out, odd_w_in, odd_q_norm_g, odd_kv_norm_g, odd_w_uq, odd_w_ukv,
           odd_w_out, ln1_g, ln1_b, mlp_w1, mlp_w2, ln2_g, ln2_b):
    bsz, seq, d = x.shape
    depth = ln1_g.shape[0]
    alpha = (2 * depth) ** 0.25
    for layer in range(depth):
        j = layer // 2
        g1, b1 = ln1_g[layer].reshape(1, d), ln1_b[layer].reshape(1, d)
        if layer % 2 == 0:
            x = _even_layer(x, even_w_in[j], even_sinks[j], even_w_out[j], g1, b1, alpha)
        else:
            x = _odd_layer(x, odd_w_in[j], odd_q_norm_g[j], odd_kv_norm_g[j], odd_w_uq[j], odd_w_ukv[j],
                           odd_w_out[j], g1, b1, alpha)
        x = _mlp_ln(x.reshape(bsz * seq, d), mlp_w1[layer].astype(BF16), mlp_w2[layer].astype(BF16),
                    ln2_g[layer].reshape(1, d), ln2_b[layer].reshape(1, d), alpha=alpha, tm=512, tf=512
                    ).reshape(bsz, seq, d)
    return x
```

```python
import functools
import math

import jax
import jax.numpy as jnp
from jax import lax
from jax.experimental import pallas as pl
from jax.experimental.pallas import tpu as pltpu

LANE = 128
V7X_VMEM_BYTES = 64 * 1024 * 1024
VMEM_LIMIT = V7X_VMEM_BYTES - 8 * 1024 * 1024

HEAD_DIM = 64
A_Q_HEADS = 16
A_KV_HEADS = 2
A_WINDOW = 128
B_HEADS = 8
B_PATTERNS = ((128, 1), (512, 4), (2048, 16))
C_HEADS = 16
D_HEADS = 16
D_Q_RANK = 512
D_KV_RANK = 256
D_NOPE = 64
D_ROPE = 32
D_V = 64
ROPE_BASE = 10000.0
LN_EPS = 1e-5
RMS_EPS = 1e-6
BLOCK = 128
NEG = -1e30

A_Q_W = A_Q_HEADS * HEAD_DIM
A_KV_W = A_KV_HEADS * HEAD_DIM
B_W = B_HEADS * HEAD_DIM
C_W = C_HEADS * HEAD_DIM

BF16 = jnp.bfloat16
F32 = jnp.float32


def _cparams(sem):
    return pltpu.CompilerParams(dimension_semantics=sem, vmem_limit_bytes=VMEM_LIMIT)


def _dot(a, b):
    return jnp.dot(a, b, preferred_element_type=F32)


def _dot_nt(a, b):
    return lax.dot_general(a, b, (((1,), (1,)), ((), ())), preferred_element_type=F32)


def _layer_norm(r, g, b):
    mu = jnp.mean(r, axis=-1, keepdims=True)
    rc = r - mu
    var = jnp.mean(rc * rc, axis=-1, keepdims=True)
    return rc * lax.rsqrt(var + LN_EPS) * g + b


def _proj_kernel(x_ref, w_ref, o_ref):
    xb = x_ref[0].astype(BF16)
    acc = _dot(xb, w_ref[...])
    for c in range(o_ref.shape[1]):
        o_ref[0, c] = acc[:, c * LANE:(c + 1) * LANE].astype(o_ref.dtype)


def _proj(x, w, *, tm, n_split):
    bsz, seq, d = x.shape
    tiles = w.shape[1] // LANE
    tps = tiles // n_split
    return pl.pallas_call(
        _proj_kernel,
        grid=(n_split, bsz, seq // tm),
        in_specs=[pl.BlockSpec((1, tm, d), lambda n, b, s: (b, s, 0)),
                  pl.BlockSpec((d, tps * LANE), lambda n, b, s: (0, n))],
        out_specs=pl.BlockSpec((1, tps, tm, LANE), lambda n, b, s: (b, n, s, 0)),
        out_shape=jax.ShapeDtypeStruct((bsz, tiles, seq, LANE), BF16),
        compiler_params=_cparams(("arbitrary", "arbitrary", "arbitrary")),
        name="proj",
    )(x, w)


def _split_heads(q2, lo):
    zero = jnp.zeros_like(q2)
    return jnp.concatenate([jnp.where(lo, q2, zero), jnp.where(lo, zero, q2)], axis=0)


def _merge_heads(o, lo, t):
    return jnp.where(lo, o[:t], o[t:])


def _lanes(x, n):
    return x if n == LANE else jnp.concatenate([x] * (n // LANE), axis=1)


def _row_stat(x):
    return jnp.broadcast_to(x, (x.shape[0], LANE))


def _rel(t, n):
    row = lax.broadcasted_iota(jnp.int32, (2 * t, n), 0)
    col = lax.broadcasted_iota(jnp.int32, (2 * t, n), 1)
    return jnp.where(row >= t, row - t, row) - col


def _per_head(t, n, even, odd):
    row = lax.broadcasted_iota(jnp.int32, (2 * t, n), 0)
    return jnp.where(row >= t, jnp.asarray(odd, F32), jnp.asarray(even, F32))


def _online_softmax_steps(ss, vs, m_ref, l_ref, acc_ref):
    ps, alphas = [], []
    for g, s in enumerate(ss):
        m_prev = m_ref[g]
        m_new = jnp.maximum(m_prev, _row_stat(jnp.max(s, axis=-1, keepdims=True)))
        alpha = jnp.exp(m_prev - m_new)
        p = jnp.exp(s - _lanes(m_new, s.shape[1]))
        l_ref[g] = alpha * l_ref[g] + _row_stat(jnp.sum(p, axis=-1, keepdims=True))
        m_ref[g] = m_new
        ps.append(p.astype(BF16))
        alphas.append(alpha)
    for g, (p, v) in enumerate(zip(ps, vs)):
        acc_ref[g] = alphas[g] * acc_ref[g] + _dot(p, v)


def _window_start(i, back, n_blocks, seq_blocks):
    return jnp.clip(i - back, 0, seq_blocks - n_blocks)


def _attn_a_kernel(sinks_ref, q_ref, k_ref, v_ref, o_ref, *, t):
    i = pl.program_id(1)
    n_tiles = q_ref.shape[1]
    seq_blocks = k_ref.shape[2] // t
    win_blocks = min(2, seq_blocks)
    n = win_blocks * t
    heads_per_kv = A_Q_HEADS // A_KV_HEADS
    lo = lax.broadcasted_iota(jnp.int32, (1, LANE), 1) < HEAD_DIM
    start = _window_start(i, 1, win_blocks, seq_blocks)
    rows = pl.ds(pl.multiple_of(start * t, t), n)
    dist = _rel(t, n) + (i - start) * t
    distf = dist.astype(F32)
    valid = (dist >= 0) & (dist < A_WINDOW)
    hi_half = lax.broadcasted_iota(jnp.int32, (2 * t, LANE), 0) >= t
    kv = [(k_ref[0, kh, rows, :], v_ref[0, kh, rows, :]) for kh in range(A_KV_HEADS)]

    for hp in range(n_tiles):
        k, v = kv[(2 * hp) // heads_per_kv]
        q = _split_heads(q_ref[0, hp] * jnp.asarray(1.0 / math.sqrt(HEAD_DIM), BF16), lo)
        slope = _per_head(t, n, 2.0 ** (-(2 * hp + 1) * 8.0 / A_Q_HEADS), 2.0 ** (-(2 * hp + 2) * 8.0 / A_Q_HEADS))
        sink = jnp.where(hi_half, sinks_ref[2 * hp + 1], sinks_ref[2 * hp])
        s = jnp.where(valid, _dot_nt(q, k) - slope * distf, NEG)
        m = jnp.maximum(_row_stat(jnp.max(s, axis=-1, keepdims=True)), sink)
        p = jnp.exp(s - _lanes(m, n))
        denom = _row_stat(jnp.sum(p, axis=-1, keepdims=True)) + jnp.exp(sink - m)
        o = _dot(p.astype(BF16), v) / denom
        o_ref[0, hp] = _merge_heads(o, lo, t).astype(o_ref.dtype)


def _attn_a(h, sinks, *, q_tile0, k_tile0, v_tile0):
    bsz, _, seq, _ = h.shape
    t = A_WINDOW
    nq = A_Q_HEADS // 2
    return pl.pallas_call(
        functools.partial(_attn_a_kernel, t=t),
        grid=(bsz, seq // t),
        in_specs=[pl.BlockSpec(memory_space=pltpu.SMEM),
                  pl.BlockSpec((1, nq, t, LANE), lambda b, i: (b, q_tile0 // nq, i, 0)),
                  pl.BlockSpec((1, A_KV_HEADS, seq, LANE), lambda b, i: (b, k_tile0 // A_KV_HEADS, 0, 0)),
                  pl.BlockSpec((1, A_KV_HEADS, seq, LANE), lambda b, i: (b, v_tile0 // A_KV_HEADS, 0, 0))],
        out_specs=pl.BlockSpec((1, nq, t, LANE), lambda b, i: (b, 0, i, 0)),
        out_shape=jax.ShapeDtypeStruct((bsz, nq, seq, LANE), BF16),
        compiler_params=_cparams(("arbitrary", "arbitrary")),
        name="attn_a",
    )(sinks, h, h, h)


def _attn_b_kernel(q0_ref, k0_ref, v0_ref, q1_ref, k1_ref, v1_ref, q2_ref, k2_ref, v2_ref,
                   o_ref, m_ref, l_ref, acc_ref, *, t, tk):
    i = pl.program_id(1)
    n_tiles = q0_ref.shape[1]
    seq = k0_ref.shape[2]
    seq_blocks = seq // t
    lo = lax.broadcasted_iota(jnp.int32, (1, LANE), 1) < HEAD_DIM
    groups = ((q0_ref, k0_ref, v0_ref), (q1_ref, k1_ref, v1_ref), (q2_ref, k2_ref, v2_ref))
    qscale = jnp.asarray(1.0 / math.sqrt(HEAD_DIM), BF16)
    slopes = [(2.0 ** (-(2 * hp + 1) * 8.0 / B_HEADS), 2.0 ** (-(2 * hp + 2) * 8.0 / B_HEADS))
              for hp in range(n_tiles)]

    bounded = []
    for (q_ref, k_ref, v_ref), (window, dil) in zip(groups, B_PATTERNS):
        back = window // t
        if back >= seq_blocks:
            continue
        win_blocks = min(back + 1, seq_blocks)
        n = win_blocks * t
        start = _window_start(i, back, win_blocks, seq_blocks)
        dist = _rel(t, n) + (i - start) * t
        valid = (dist >= 0) & (dist <= window) & ((dist & (dil - 1)) == 0)
        bounded.append((q_ref, k_ref, v_ref, pl.ds(pl.multiple_of(start * t, t), n), n, dist.astype(F32), valid))

    for hp in range(n_tiles):
        parts = []
        for q_ref, k_ref, v_ref, rows, n, distf, valid in bounded:
            q = _split_heads(q_ref[0, hp] * qscale, lo)
            s = _dot_nt(q, k_ref[0, hp, rows, :]) - _per_head(t, n, *slopes[hp]) * distf
            parts.append((jnp.where(valid, s, NEG), v_ref[0, hp, rows, :], n))
        m = jnp.full((2 * t, LANE), NEG, F32)
        for s, _, _ in parts:
            m = jnp.maximum(m, _row_stat(jnp.max(s, axis=-1, keepdims=True)))
        l = jnp.zeros((2 * t, LANE), F32)
        acc = jnp.zeros((2 * t, LANE), F32)
        for s, v, n in parts:
            p = jnp.exp(s - _lanes(m, n))
            l = l + _row_stat(jnp.sum(p, axis=-1, keepdims=True))
            acc = acc + _dot(p.astype(BF16), v)
        m_ref[hp], l_ref[hp], acc_ref[hp] = m, l, acc

    for (q_ref, k_ref, v_ref), (window, dil) in zip(groups, B_PATTERNS):
        if window // t < seq_blocks:
            continue
        rel = _rel(t, tk)
        qs = [_split_heads(q_ref[0, hp] * qscale, lo) for hp in range(n_tiles)]

        def body(j, c, k_ref=k_ref, v_ref=v_ref, rel=rel, qs=qs, dil=dil):
            rows = pl.ds(pl.multiple_of(j * tk, tk), tk)
            dist = rel + (i * t - j * tk)
            valid = (dist >= 0) & ((dist & (dil - 1)) == 0)
            distf = dist.astype(F32)
            ss = [jnp.where(valid, _dot_nt(qs[hp], k_ref[0, hp, rows, :]) - _per_head(t, tk, *slopes[hp]) * distf, NEG)
                  for hp in range(n_tiles)]
            _online_softmax_steps(ss, [v_ref[0, hp, rows, :] for hp in range(n_tiles)], m_ref, l_ref, acc_ref)
            return c

        lax.fori_loop(0, (i * t) // tk + 1, body, 0)

    for hp in range(n_tiles):
        o_ref[0, hp] = _merge_heads(acc_ref[hp] / l_ref[hp], lo, t).astype(o_ref.dtype)


def _attn_b(h, *, tile0):
    bsz, _, seq, _ = h.shape
    t = BLOCK
    n = B_HEADS // 2
    in_specs = []
    for g in range(len(B_PATTERNS)):
        base = tile0 + 3 * n * g
        in_specs.append(pl.BlockSpec((1, n, t, LANE), lambda b, i, base=base: (b, base // n, i, 0)))
        in_specs.append(pl.BlockSpec((1, n, seq, LANE), lambda b, i, base=base: (b, base // n + 1, 0, 0)))
        in_specs.append(pl.BlockSpec((1, n, seq, LANE), lambda b, i, base=base: (b, base // n + 2, 0, 0)))
    return pl.pallas_call(
        functools.partial(_attn_b_kernel, t=t, tk=2 * t),
        grid=(bsz, seq // t),
        in_specs=in_specs,
        out_specs=pl.BlockSpec((1, n, t, LANE), lambda b, i: (b, 0, i, 0)),
        out_shape=jax.ShapeDtypeStruct((bsz, n, seq, LANE), BF16),
        scratch_shapes=[pltpu.VMEM((n, 2 * t, LANE), F32)] * 3,
        compiler_params=_cparams(("arbitrary", "arbitrary")),
        name="attn_b",
    )(*([h] * 9))


def _attn_c_kernel(q_ref, k_ref, v_ref, o_ref, c_ref, acc_ref, *, t, group):
    i = pl.program_id(1)
    n_tiles = q_ref.shape[1]
    lo = lax.broadcasted_iota(jnp.int32, (1, LANE), 1) < HEAD_DIM
    strict = _rel(t, t) > 0
    jj = lax.broadcasted_iota(jnp.int32, (t, t), 0)
    ss = lax.broadcasted_iota(jnp.int32, (t, t), 1)
    tri = jnp.where(jj > ss, 1.0, 0.0).astype(BF16)
    qscale = jnp.asarray(1.0 / math.sqrt(HEAD_DIM), BF16)

    def per_group(gi, carry):
        hps = [gi * group + g for g in range(group)]
        qs = [_split_heads(q_ref[0, hp] * qscale, lo) for hp in hps]
        for g in range(group):
            c_ref[g] = jnp.zeros((2 * t, LANE), F32)
            acc_ref[g] = jnp.zeros((2 * t, LANE), F32)

        def step(kb, diagonal):
            rows = pl.ds(pl.multiple_of(kb * t, t), t)
            zs = [_dot_nt(qs[g], k_ref[0, hp, rows, :]) for g, hp in enumerate(hps)]
            log_betas, log_keeps, splits = [], [], []
            for z in zs:
                log_beta = jnp.minimum(z, 0.0) - jnp.log(1.0 + jnp.exp(-jnp.abs(z)))
                log_keep = log_beta - z
                if diagonal:
                    log_keep = jnp.where(strict, log_keep, 0.0)
                keep_hi = log_keep.astype(BF16)
                keep_lo = (log_keep - keep_hi.astype(F32)).astype(BF16)
                log_betas.append(log_beta)
                log_keeps.append(log_keep)
                splits.append((keep_hi, keep_lo))
            afters = [_dot(keep_hi, tri) + _dot(keep_lo, tri) for keep_hi, keep_lo in splits]
            ws = []
            for g in range(group):
                w = jnp.exp(log_betas[g] + afters[g] + _lanes(c_ref[g], t))
                if diagonal:
                    w = jnp.where(strict, w, 0.0)
                ws.append(w.astype(BF16))
                c_ref[g] += _row_stat(jnp.sum(log_keeps[g], axis=-1, keepdims=True))
            for g, hp in enumerate(hps):
                acc_ref[g] += _dot(ws[g], v_ref[0, hp, rows, :])

        step(i, True)

        def body(j, c):
            step(i - 1 - j, False)
            return c

        lax.fori_loop(0, i, body, 0)
        for g, hp in enumerate(hps):
            o_ref[0, hp] = _merge_heads(acc_ref[g], lo, t).astype(o_ref.dtype)
        return carry

    lax.fori_loop(0, n_tiles // group, per_group, 0)


def _attn_c(h, *, t, group):
    bsz, _, seq, _ = h.shape
    n = C_HEADS // 2
    return pl.pallas_call(
        functools.partial(_attn_c_kernel, t=t, group=group),
        grid=(bsz, seq // t),
        in_specs=[pl.BlockSpec((1, n, t, LANE), lambda b, i: (b, 0, i, 0)),
                  pl.BlockSpec((1, n, seq, LANE), lambda b, i: (b, 1, 0, 0)),
                  pl.BlockSpec((1, n, seq, LANE), lambda b, i: (b, 2, 0, 0))],
        out_specs=pl.BlockSpec((1, n, t, LANE), lambda b, i: (b, 0, i, 0)),
        out_shape=jax.ShapeDtypeStruct((bsz, n, seq, LANE), BF16),
        scratch_shapes=[pltpu.VMEM((group, 2 * t, LANE), F32)] * 2,
        compiler_params=_cparams(("arbitrary", "arbitrary")),
        name="attn_c",
    )(h, h, h)


def _rope(x, cos_t, sin_t, first_half):
    partner = jnp.where(first_half, pltpu.roll(x, LANE - D_ROPE // 2, 1), pltpu.roll(x, D_ROPE // 2, 1))
    return x * cos_t + partner * sin_t


def _mla_proj_kernel(x_ref, wl_ref, gq_ref, gkv_ref, wuq_ref, wukv_ref, cos_ref, sin_ref,
                     qn_ref, qr_ref, kn_ref, vd_ref, kr_ref):
    xb = x_ref[0].astype(BF16)
    lat = _dot(xb, wl_ref[...])
    cq = lat[:, :D_Q_RANK]
    ckv = lat[:, D_Q_RANK:D_Q_RANK + D_KV_RANK]
    kr = lat[:, D_Q_RANK + D_KV_RANK:]
    cqn = cq * lax.rsqrt(jnp.mean(cq * cq, axis=-1, keepdims=True) + RMS_EPS) * gq_ref[...]
    ckvn = ckv * lax.rsqrt(jnp.mean(ckv * ckv, axis=-1, keepdims=True) + RMS_EPS) * gkv_ref[...]
    q = _dot(cqn.astype(BF16), wuq_ref[...])
    kv = _dot(ckvn.astype(BF16), wukv_ref[...])
    cos_t = cos_ref[...]
    sin_t = sin_ref[...]
    first_half = (lax.broadcasted_iota(jnp.int32, (1, LANE), 1) % D_ROPE) < D_ROPE // 2
    n_nope = qn_ref.shape[1]
    for c in range(n_nope):
        qn_ref[0, c] = q[:, c * LANE:(c + 1) * LANE].astype(BF16)
        kn_ref[0, c] = kv[:, c * LANE:(c + 1) * LANE].astype(BF16)
        vd_ref[0, c] = kv[:, (n_nope + c) * LANE:(n_nope + c + 1) * LANE].astype(BF16)
    for c in range(qr_ref.shape[1]):
        qr = q[:, (n_nope + c) * LANE:(n_nope + c + 1) * LANE]
        qr_ref[0, c] = _rope(qr, cos_t, sin_t, first_half).astype(BF16)
    kr_ref[0] = _rope(kr, cos_t, sin_t, first_half).astype(BF16)


def _mla_proj(x, w_lat, gq, gkv, w_uq, w_ukv, cos_t, sin_t, *, tm):
    bsz, seq, d = x.shape
    n_nope = D_HEADS * D_NOPE // LANE
    n_rope = D_HEADS * D_ROPE // LANE
    full = lambda a: pl.BlockSpec(a.shape, lambda b, s: (0,) * a.ndim)
    tile_out = lambda n: pl.BlockSpec((1, n, tm, LANE), lambda b, s: (b, 0, s, 0))
    tile_shape = lambda n: jax.ShapeDtypeStruct((bsz, n, seq, LANE), BF16)
    return pl.pallas_call(
        _mla_proj_kernel,
        grid=(bsz, seq // tm),
        in_specs=[pl.BlockSpec((1, tm, d), lambda b, s: (b, s, 0)),
                  full(w_lat), full(gq), full(gkv), full(w_uq), full(w_ukv),
                  pl.BlockSpec((tm, LANE), lambda b, s: (s, 0)),
                  pl.BlockSpec((tm, LANE), lambda b, s: (s, 0))],
        out_specs=[tile_out(n_nope), tile_out(n_rope), tile_out(n_nope), tile_out(n_nope),
                   pl.BlockSpec((1, tm, LANE), lambda b, s: (b, s, 0))],
        out_shape=[tile_shape(n_nope), tile_shape(n_rope), tile_shape(n_nope), tile_shape(n_nope),
                   jax.ShapeDtypeStruct((bsz, seq, LANE), BF16)],
        compiler_params=_cparams(("arbitrary", "arbitrary")),
        name="mla_proj",
    )(x, w_lat, gq, gkv, w_uq, w_ukv, cos_t, sin_t)


def _attn_d_kernel(qn_ref, qr_ref, kn_ref, kr_ref, vd_ref, o_ref, m_ref, l_ref, acc_ref, *, t, group):
    i = pl.program_id(1)
    n_tiles = qn_ref.shape[1]
    lane = lax.broadcasted_iota(jnp.int32, (1, LANE), 1)
    lo = lane < D_NOPE
    lane_q = lane // D_ROPE
    causal = _rel(t, t) >= 0
    scale = 1.0 / math.sqrt(D_NOPE + D_ROPE)

    def per_group(gi, carry):
        hps = [gi * group + g for g in range(group)]
        qs = []
        for g, hp in enumerate(hps):
            qn = _split_heads(qn_ref[0, hp], lo)
            qr2 = qr_ref[0, gi * (group // 2) + g // 2]
            quarter = (g % 2) * 2
            zero = jnp.zeros_like(qr2)
            qr = jnp.concatenate([jnp.where(lane_q == quarter, qr2, zero),
                                  jnp.where(lane_q == quarter + 1, qr2, zero)], axis=0)
            qs.append(jnp.concatenate([qn, qr], axis=1))
            m_ref[g] = jnp.full((2 * t, LANE), NEG, F32)
            l_ref[g] = jnp.zeros((2 * t, LANE), F32)
            acc_ref[g] = jnp.zeros((2 * t, LANE), F32)

        def step(kb, diagonal):
            rows = pl.ds(pl.multiple_of(kb * t, t), t)
            kr = kr_ref[0, rows, :]
            ss = [_dot_nt(qs[g], jnp.concatenate([kn_ref[0, hp, rows, :], kr], axis=1)) * scale
                  for g, hp in enumerate(hps)]
            if diagonal:
                ss = [jnp.where(causal, s, NEG) for s in ss]
            _online_softmax_steps(ss, [vd_ref[0, hp, rows, :] for hp in hps], m_ref, l_ref, acc_ref)

        step(i, True)

        def body(kb, c):
            step(kb, False)
            return c

        lax.fori_loop(0, i, body, 0)
        for g, hp in enumerate(hps):
            o_ref[0, hp] = _merge_heads(acc_ref[g] / l_ref[g], lo, t).astype(o_ref.dtype)
        return carry

    lax.fori_loop(0, n_tiles // group, per_group, 0)


def _attn_d(qn, qr, kn, kr, vd, *, t, group):
    bsz, n, seq, _ = qn.shape
    nr = qr.shape[1]
    return pl.pallas_call(
        functools.partial(_attn_d_kernel, t=t, group=group),
        grid=(bsz, seq // t),
        in_specs=[pl.BlockSpec((1, n, t, LANE), lambda b, i: (b, 0, i, 0)),
                  pl.BlockSpec((1, nr, t, LANE), lambda b, i: (b, 0, i, 0)),
                  pl.BlockSpec((1, n, seq, LANE), lambda b, i: (b, 0, 0, 0)),
                  pl.BlockSpec((1, seq, LANE), lambda b, i: (b, 0, 0)),
                  pl.BlockSpec((1, n, seq, LANE), lambda b, i: (b, 0, 0, 0))],
        out_specs=pl.BlockSpec((1, n, t, LANE), lambda b, i: (b, 0, i, 0)),
        out_shape=jax.ShapeDtypeStruct((bsz, n, seq, LANE), BF16),
        scratch_shapes=[pltpu.VMEM((group, 2 * t, LANE), F32)] * 3,
        compiler_params=_cparams(("arbitrary", "arbitrary")),
        name="attn_d",
    )(qn, qr, kn, kr, vd)


def _out_ln_kernel(x_ref, ya_ref, yb_ref, wa_ref, wb_ref, g_ref, b_ref, o_ref, *, alpha):
    ya = jnp.concatenate([ya_ref[0, c] for c in range(ya_ref.shape[1])], axis=1)
    yb = jnp.concatenate([yb_ref[0, c] for c in range(yb_ref.shape[1])], axis=1)
    mixed = _dot(ya, wa_ref[...]) + _dot(yb, wb_ref[...])
    o_ref[0] = _layer_norm(alpha * x_ref[0] + mixed, g_ref[...], b_ref[...])


def _out_ln(x, ya, yb, wa, wb, g, b, *, alpha, tm):
    bsz, seq, d = x.shape
    full = lambda a: pl.BlockSpec(a.shape, lambda bb, s: (0,) * a.ndim)
    return pl.pallas_call(
        functools.partial(_out_ln_kernel, alpha=alpha),
        grid=(bsz, seq // tm),
        in_specs=[pl.BlockSpec((1, tm, d), lambda bb, s: (bb, s, 0)),
                  pl.BlockSpec((1, ya.shape[1], tm, LANE), lambda bb, s: (bb, 0, s, 0)),
                  pl.BlockSpec((1, yb.shape[1], tm, LANE), lambda bb, s: (bb, 0, s, 0)),
                  full(wa), full(wb), full(g), full(b)],
        out_specs=pl.BlockSpec((1, tm, d), lambda bb, s: (bb, s, 0)),
        out_shape=jax.ShapeDtypeStruct((bsz, seq, d), F32),
        compiler_params=_cparams(("arbitrary", "arbitrary")),
        name="out_ln",
    )(x, ya, yb, wa, wb, g, b)


def _mlp_ln_kernel(x_ref, w1_ref, w2_ref, g_ref, b_ref, o_ref, xb_ref, acc_ref, *, alpha):
    f = pl.program_id(1)

    @pl.when(f == 0)
    def _():
        xb_ref[...] = x_ref[...].astype(BF16)
        acc_ref[...] = jnp.zeros_like(acc_ref)

    hid = jnp.maximum(_dot(xb_ref[...], w1_ref[...]), 0.0)
    acc_ref[...] += _dot((hid * hid).astype(BF16), w2_ref[...])

    @pl.when(f == pl.num_programs(1) - 1)
    def _():
        o_ref[...] = _layer_norm(alpha * x_ref[...] + acc_ref[...], g_ref[...], b_ref[...])


def _mlp_ln(x, w1, w2, g, b, *, alpha, tm, tf):
    m, d = x.shape
    ff = w1.shape[1]
    return pl.pallas_call(
        functools.partial(_mlp_ln_kernel, alpha=alpha),
        grid=(m // tm, ff // tf),
        in_specs=[pl.BlockSpec((tm, d), lambda i, f: (i, 0)),
                  pl.BlockSpec((d, tf), lambda i, f: (0, f)),
                  pl.BlockSpec((tf, d), lambda i, f: (f, 0)),
                  pl.BlockSpec((1, d), lambda i, f: (0, 0)),
                  pl.BlockSpec((1, d), lambda i, f: (0, 0))],
        out_specs=pl.BlockSpec((tm, d), lambda i, f: (i, 0)),
        out_shape=jax.ShapeDtypeStruct((m, d), F32),
        scratch_shapes=[pltpu.VMEM((tm, d), BF16), pltpu.VMEM((tm, d), F32)],
        compiler_params=_cparams(("arbitrary", "arbitrary")),
        name="mlp_ln",
    )(x, w1, w2, g, b)


def _even_w_in_layout(w):
    qa = w[:, :A_Q_W]
    ka = w[:, A_Q_W:A_Q_W + A_KV_W]
    va = w[:, A_Q_W + A_KV_W:A_Q_W + 2 * A_KV_W]
    hb = w[:, A_Q_W + 2 * A_KV_W:]
    dup = lambda kv: [kv[:, h * HEAD_DIM:(h + 1) * HEAD_DIM] for h in range(A_KV_HEADS) for _ in range(2)]
    return jnp.concatenate([qa, hb] + dup(ka) + dup(va), axis=1).astype(BF16)


def _rope_tables(seq):
    inv_freq = ROPE_BASE ** (-jnp.arange(0, D_ROPE, 2, dtype=F32) / D_ROPE)
    ang = jnp.arange(seq, dtype=F32)[:, None] * inv_freq[None, :]
    cos, sin = jnp.cos(ang), jnp.sin(ang)
    reps = LANE // D_ROPE
    cos_t = jnp.tile(jnp.concatenate([cos, cos], axis=1), (1, reps))
    sin_t = jnp.tile(jnp.concatenate([-sin, sin], axis=1), (1, reps))
    return cos_t, sin_t


def _even_layer(x, w_in, sinks, w_out, g, b, alpha):
    h = _proj(x, _even_w_in_layout(w_in), tm=512, n_split=2)
    n_qa = A_Q_W // LANE
    n_b = 3 * len(B_PATTERNS) * B_W // LANE
    oa = _attn_a(h, sinks, q_tile0=0, k_tile0=n_qa + n_b, v_tile0=n_qa + n_b + A_KV_HEADS)
    ob = _attn_b(h, tile0=n_qa)
    w_out = w_out.astype(BF16)
    return _out_ln(x, oa, ob, w_out[:A_Q_W], w_out[A_Q_W:], g, b, alpha=alpha, tm=512)


def _odd_layer(x, w_in, gq, gkv, w_uq, w_ukv, w_out, g, b, alpha):
    seq = x.shape[1]
    hc = _proj(x, w_in[:, :3 * C_W].astype(BF16), tm=512, n_split=1)
    oc = _attn_c(hc, t=256, group=4)
    w_lat = w_in[:, 3 * C_W:]
    kr_w = w_lat[:, D_Q_RANK + D_KV_RANK:]
    w_lat = jnp.concatenate([w_lat[:, :D_Q_RANK + D_KV_RANK]] + [kr_w] * (LANE // D_ROPE), axis=1).astype(BF16)
    uq = w_uq.reshape(D_Q_RANK, D_HEADS, D_NOPE + D_ROPE)
    uq = jnp.concatenate([uq[:, :, :D_NOPE].reshape(D_Q_RANK, -1), uq[:, :, D_NOPE:].reshape(D_Q_RANK, -1)], axis=1)
    ukv = w_ukv.reshape(D_KV_RANK, D_HEADS, D_NOPE + D_V)
    ukv = jnp.concatenate([ukv[:, :, :D_NOPE].reshape(D_KV_RANK, -1), ukv[:, :, D_NOPE:].reshape(D_KV_RANK, -1)], axis=1)
    cos_t, sin_t = _rope_tables(seq)
    qn, qr, kn, vd, kr = _mla_proj(x, w_lat, gq.reshape(1, -1), gkv.reshape(1, -1),
                                   uq.astype(BF16), ukv.astype(BF16), cos_t, sin_t, tm=512)
    od = _attn_d(qn, qr, kn, kr, vd, t=256, group=4)
    w_out = w_out.astype(BF16)
    return _out_ln(x, oc, od, w_out[:C_W], w_out[C_W:], g, b, alpha=alpha, tm=512)


def kernel(x, even_w_in, even_sinks, even_w_out, odd_w_in, odd_q_norm_g, odd_kv_norm_g, odd_w_uq, odd_w_ukv,
           odd_w_out, ln1_g, ln1_b, mlp_w1, mlp_w2, ln2_g, ln2_b):
    bsz, seq, d = x.shape
    depth = ln1_g.shape[0]
    alpha = (2 * depth) ** 0.25
    for layer in range(depth):
        j = layer // 2
        g1, b1 = ln1_g[layer].reshape(1, d), ln1_b[layer].reshape(1, d)
        if layer % 2 == 0:
            x = _even_layer(x, even_w_in[j], even_sinks[j], even_w_out[j], g1, b1, alpha)
        else:
            x = _odd_layer(x, odd_w_in[j], odd_q_norm_g[j], odd_kv_norm_g[j], odd_w_uq[j], odd_w_ukv[j],
                           odd_w_out[j], g1, b1, alpha)
        x = _mlp_ln(x.reshape(bsz * seq, d), mlp_w1[layer].astype(BF16), mlp_w2[layer].astype(BF16),
                    ln2_g[layer].reshape(1, d), ln2_b[layer].reshape(1, d), alpha=alpha, tm=512, tf=512
                    ).reshape(bsz, seq, d)
    return x
```

```python
import functools
import math

import jax
import jax.numpy as jnp
from jax import lax
from jax.experimental import pallas as pl
from jax.experimental.pallas import tpu as pltpu

LANE = 128
V7X_VMEM_BYTES = 64 * 1024 * 1024
VMEM_LIMIT = V7X_VMEM_BYTES - 8 * 1024 * 1024

HEAD_DIM = 64
A_Q_HEADS = 16
A_KV_HEADS = 2
A_WINDOW = 128
B_HEADS = 8
B_PATTERNS = ((128, 1), (512, 4), (2048, 16))
C_HEADS = 16
D_HEADS = 16
D_Q_RANK = 512
D_KV_RANK = 256
D_NOPE = 64
D_ROPE = 32
D_V = 64
ROPE_BASE = 10000.0
LN_EPS = 1e-5
RMS_EPS = 1e-6
BLOCK = 128
NEG = -1e30

A_Q_W = A_Q_HEADS * HEAD_DIM
A_KV_W = A_KV_HEADS * HEAD_DIM
B_W = B_HEADS * HEAD_DIM
C_W = C_HEADS * HEAD_DIM

BF16 = jnp.bfloat16
F32 = jnp.float32


def _cparams(sem):
    return pltpu.CompilerParams(dimension_semantics=sem, vmem_limit_bytes=VMEM_LIMIT)


def _dot(a, b):
    return jnp.dot(a, b, preferred_element_type=F32)


def _dot_nt(a, b):
    return lax.dot_general(a, b, (((1,), (1,)), ((), ())), preferred_element_type=F32)


def _layer_norm(r, g, b):
    mu = jnp.mean(r, axis=-1, keepdims=True)
    rc = r - mu
    var = jnp.mean(rc * rc, axis=-1, keepdims=True)
    return rc * lax.rsqrt(var + LN_EPS) * g + b


def _proj_kernel(x_ref, w_ref, o_ref):
    xb = x_ref[0].astype(BF16)
    acc = _dot(xb, w_ref[...])
    for c in range(o_ref.shape[1]):
        o_ref[0, c] = acc[:, c * LANE:(c + 1) * LANE].astype(o_ref.dtype)


def _proj(x, w, *, tm, n_split):
    bsz, seq, d = x.shape
    tiles = w.shape[1] // LANE
    tps = tiles // n_split
    return pl.pallas_call(
        _proj_kernel,
        grid=(n_split, bsz, seq // tm),
        in_specs=[pl.BlockSpec((1, tm, d), lambda n, b, s: (b, s, 0)),
                  pl.BlockSpec((d, tps * LANE), lambda n, b, s: (0, n))],
        out_specs=pl.BlockSpec((1, tps, tm, LANE), lambda n, b, s: (b, n, s, 0)),
        out_shape=jax.ShapeDtypeStruct((bsz, tiles, seq, LANE), BF16),
        compiler_params=_cparams(("arbitrary", "arbitrary", "arbitrary")),
        name="proj",
    )(x, w)


def _split_heads(q2, lo):
    zero = jnp.zeros_like(q2)
    return jnp.concatenate([jnp.where(lo, q2, zero), jnp.where(lo, zero, q2)], axis=0)


def _merge_heads(o, lo, t):
    return jnp.where(lo, o[:t], o[t:])


def _lanes(x, n):
    return x if n == LANE else jnp.concatenate([x] * (n // LANE), axis=1)


def _row_stat(x):
    return jnp.broadcast_to(x, (x.shape[0], LANE))


def _fold_lanes(p):
    out = p[:, :LANE]
    for c in range(1, p.shape[1] // LANE):
        out = out + p[:, c * LANE:(c + 1) * LANE]
    return out


def _finish(acc, l_part):
    return acc / _row_stat(jnp.sum(l_part, axis=-1, keepdims=True))


def _skewed(n_chains, stages):
    state = [dict() for _ in range(n_chains)]
    for step in range(n_chains + len(stages) - 1):
        for g in range(n_chains):
            s = step - g
            if 0 <= s < len(stages):
                stages[s](g, state[g])


def _rel(t, n):
    row = lax.broadcasted_iota(jnp.int32, (2 * t, n), 0)
    col = lax.broadcasted_iota(jnp.int32, (2 * t, n), 1)
    return jnp.where(row >= t, row - t, row) - col


def _per_head(t, n, even, odd):
    row = lax.broadcasted_iota(jnp.int32, (2 * t, n), 0)
    return jnp.where(row >= t, jnp.asarray(odd, F32), jnp.asarray(even, F32))


def _online_softmax_steps(ss, vs, m_ref, l_ref, acc_ref, exp=jnp.exp):
    def scores(g, st):
        st["s"] = ss[g]()

    def probabilities(g, st):
        s = st.pop("s")
        m_prev = m_ref[g]
        m_new = jnp.maximum(m_prev, _row_stat(jnp.max(s, axis=-1, keepdims=True)))
        alpha = exp(m_prev - m_new)
        p = exp(s - _lanes(m_new, s.shape[1]))
        l_ref[g] = alpha * l_ref[g] + _fold_lanes(p)
        m_ref[g] = m_new
        st["p"] = p.astype(BF16)
        st["alpha"] = alpha

    def values(g, st):
        acc_ref[g] = st.pop("alpha") * acc_ref[g] + _dot(st.pop("p"), vs[g]())

    state = [dict() for _ in ss]
    for stage in (scores, probabilities, values):
        for g, st in enumerate(state):
            stage(g, st)


def _window_start(i, back, n_blocks, seq_blocks):
    return jnp.clip(i - back, 0, seq_blocks - n_blocks)


def _attn_a_kernel(sinks_ref, q_ref, k_ref, v_ref, o_ref, *, t):
    i = pl.program_id(1)
    n_tiles = q_ref.shape[1]
    seq_blocks = k_ref.shape[2] // t
    win_blocks = min(2, seq_blocks)
    n = win_blocks * t
    heads_per_kv = A_Q_HEADS // A_KV_HEADS
    lo = lax.broadcasted_iota(jnp.int32, (1, LANE), 1) < HEAD_DIM
    start = _window_start(i, 1, win_blocks, seq_blocks)
    rows = pl.ds(pl.multiple_of(start * t, t), n)
    dist = _rel(t, n) + (i - start) * t
    distf = dist.astype(F32)
    valid = (dist >= 0) & (dist < A_WINDOW)
    hi_half = lax.broadcasted_iota(jnp.int32, (2 * t, LANE), 0) >= t
    kv = [(k_ref[0, kh, rows, :], v_ref[0, kh, rows, :]) for kh in range(A_KV_HEADS)]

    def scores(hp, st):
        q = _split_heads(q_ref[0, hp] * jnp.asarray(1.0 / math.sqrt(HEAD_DIM), BF16), lo)
        slope = _per_head(t, n, 2.0 ** (-(2 * hp + 1) * 8.0 / A_Q_HEADS), 2.0 ** (-(2 * hp + 2) * 8.0 / A_Q_HEADS))
        st["s"] = jnp.where(valid, _dot_nt(q, kv[(2 * hp) // heads_per_kv][0]) - slope * distf, NEG)

    def probabilities(hp, st):
        s = st.pop("s")
        sink = jnp.where(hi_half, sinks_ref[2 * hp + 1], sinks_ref[2 * hp])
        m = jnp.maximum(_row_stat(jnp.max(s, axis=-1, keepdims=True)), sink)
        p = jnp.exp(s - _lanes(m, n))
        st["denom"] = _row_stat(jnp.sum(p, axis=-1, keepdims=True)) + jnp.exp(sink - m)
        st["p"] = p.astype(BF16)

    def values(hp, st):
        o = _dot(st.pop("p"), kv[(2 * hp) // heads_per_kv][1]) / st.pop("denom")
        o_ref[0, hp] = _merge_heads(o, lo, t).astype(o_ref.dtype)

    state = [dict() for _ in range(n_tiles)]
    for stage in (scores, probabilities, values):
        for hp in range(n_tiles):
            stage(hp, state[hp])


def _attn_a(h, sinks, *, q_tile0, k_tile0, v_tile0):
    bsz, _, seq, _ = h.shape
    t = A_WINDOW
    nq = A_Q_HEADS // 2
    return pl.pallas_call(
        functools.partial(_attn_a_kernel, t=t),
        grid=(bsz, seq // t),
        in_specs=[pl.BlockSpec(memory_space=pltpu.SMEM),
                  pl.BlockSpec((1, nq, t, LANE), lambda b, i: (b, q_tile0 // nq, i, 0)),
                  pl.BlockSpec((1, A_KV_HEADS, seq, LANE), lambda b, i: (b, k_tile0 // A_KV_HEADS, 0, 0)),
                  pl.BlockSpec((1, A_KV_HEADS, seq, LANE), lambda b, i: (b, v_tile0 // A_KV_HEADS, 0, 0))],
        out_specs=pl.BlockSpec((1, nq, t, LANE), lambda b, i: (b, 0, i, 0)),
        out_shape=jax.ShapeDtypeStruct((bsz, nq, seq, LANE), BF16),
        compiler_params=_cparams(("arbitrary", "arbitrary")),
        name="attn_a",
    )(sinks, h, h, h)


def _attn_b_kernel(q0_ref, k0_ref, v0_ref, q1_ref, k1_ref, v1_ref, q2_ref, k2_ref, v2_ref,
                   o_ref, m_ref, l_ref, acc_ref, *, t, tk):
    i = pl.program_id(1)
    n_tiles = q0_ref.shape[1]
    seq = k0_ref.shape[2]
    seq_blocks = seq // t
    lo = lax.broadcasted_iota(jnp.int32, (1, LANE), 1) < HEAD_DIM
    groups = ((q0_ref, k0_ref, v0_ref), (q1_ref, k1_ref, v1_ref), (q2_ref, k2_ref, v2_ref))
    qscale = jnp.asarray(1.0 / math.sqrt(HEAD_DIM), BF16)
    slopes = [(2.0 ** (-(2 * hp + 1) * 8.0 / B_HEADS), 2.0 ** (-(2 * hp + 2) * 8.0 / B_HEADS))
              for hp in range(n_tiles)]

    bounded = []
    for (q_ref, k_ref, v_ref), (window, dil) in zip(groups, B_PATTERNS):
        back = window // t
        if back >= seq_blocks:
            continue
        win_blocks = min(back + 1, seq_blocks)
        n = win_blocks * t
        start = _window_start(i, back, win_blocks, seq_blocks)
        dist = _rel(t, n) + (i - start) * t
        valid = (dist >= 0) & (dist <= window) & ((dist & (dil - 1)) == 0)
        bounded.append((q_ref, k_ref, v_ref, pl.ds(pl.multiple_of(start * t, t), n), n, dist.astype(F32), valid))

    def scores(hp, st):
        st["s"] = []
        for q_ref, k_ref, v_ref, rows, n, distf, valid in bounded:
            q = _split_heads(q_ref[0, hp] * qscale, lo)
            s = _dot_nt(q, k_ref[0, hp, rows, :]) - _per_head(t, n, *slopes[hp]) * distf
            st["s"].append(jnp.where(valid, s, NEG))

    def probabilities(hp, st):
        m = jnp.full((2 * t, LANE), NEG, F32)
        for s in st["s"]:
            m = jnp.maximum(m, _row_stat(jnp.max(s, axis=-1, keepdims=True)))
        l = jnp.zeros((2 * t, LANE), F32)
        st["p"] = []
        for s in st.pop("s"):
            p = jnp.exp(s - _lanes(m, s.shape[1]))
            l = l + _fold_lanes(p)
            st["p"].append(p.astype(BF16))
        m_ref[hp], l_ref[hp] = m, l

    def values(hp, st):
        acc = jnp.zeros((2 * t, LANE), F32)
        for p, (_, _, v_ref, rows, _, _, _) in zip(st.pop("p"), bounded):
            acc = acc + _dot(p, v_ref[0, hp, rows, :])
        acc_ref[hp] = acc

    state = [dict() for _ in range(n_tiles)]
    for stage in (scores, probabilities, values):
        for hp in range(n_tiles):
            stage(hp, state[hp])

    for (q_ref, k_ref, v_ref), (window, dil) in zip(groups, B_PATTERNS):
        if window // t < seq_blocks:
            continue
        rel = _rel(t, tk)
        qs = [_split_heads(q_ref[0, hp] * qscale, lo) for hp in range(n_tiles)]

        def body(j, c, k_ref=k_ref, v_ref=v_ref, rel=rel, qs=qs, dil=dil):
            rows = pl.ds(pl.multiple_of(j * tk, tk), tk)
            dist = rel + (i * t - j * tk)
            valid = (dist >= 0) & ((dist & (dil - 1)) == 0)
            distf = dist.astype(F32)
            def score_fn(hp):
                return lambda: jnp.where(
                    valid, _dot_nt(qs[hp], k_ref[0, hp, rows, :]) - _per_head(t, tk, *slopes[hp]) * distf, NEG)

            _online_softmax_steps([score_fn(hp) for hp in range(n_tiles)],
                                  [functools.partial(lambda hp: v_ref[0, hp, rows, :], hp) for hp in range(n_tiles)],
                                  m_ref, l_ref, acc_ref)
            return c

        lax.fori_loop(0, (i * t) // tk + 1, body, 0)

    for hp in range(n_tiles):
        o_ref[0, hp] = _merge_heads(_finish(acc_ref[hp], l_ref[hp]), lo, t).astype(o_ref.dtype)


def _attn_b(h, *, tile0):
    bsz, _, seq, _ = h.shape
    t = BLOCK
    n = B_HEADS // 2
    in_specs = []
    for g in range(len(B_PATTERNS)):
        base = tile0 + 3 * n * g
        in_specs.append(pl.BlockSpec((1, n, t, LANE), lambda b, i, base=base: (b, base // n, i, 0)))
        in_specs.append(pl.BlockSpec((1, n, seq, LANE), lambda b, i, base=base: (b, base // n + 1, 0, 0)))
        in_specs.append(pl.BlockSpec((1, n, seq, LANE), lambda b, i, base=base: (b, base // n + 2, 0, 0)))
    return pl.pallas_call(
        functools.partial(_attn_b_kernel, t=t, tk=2 * t),
        grid=(bsz, seq // t),
        in_specs=in_specs,
        out_specs=pl.BlockSpec((1, n, t, LANE), lambda b, i: (b, 0, i, 0)),
        out_shape=jax.ShapeDtypeStruct((bsz, n, seq, LANE), BF16),
        scratch_shapes=[pltpu.VMEM((n, 2 * t, LANE), F32)] * 3,
        compiler_params=_cparams(("arbitrary", "arbitrary")),
        name="attn_b",
    )(*([h] * 9))


def _attn_c_kernel(q_ref, k_ref, v_ref, o_ref, c_ref, acc_ref, *, t, group):
    i = pl.program_id(1)
    n_tiles = q_ref.shape[1]
    lo = lax.broadcasted_iota(jnp.int32, (1, LANE), 1) < HEAD_DIM
    strict = _rel(t, t) > 0
    jj = lax.broadcasted_iota(jnp.int32, (t, t), 0)
    ss = lax.broadcasted_iota(jnp.int32, (t, t), 1)
    tri = jnp.where(jj >= ss, 1.0, 0.0).astype(BF16)
    qscale = jnp.asarray(1.0 / math.sqrt(HEAD_DIM), BF16)

    def per_group(gi, carry):
        hps = [gi * group + g for g in range(group)]
        qs = [_split_heads(q_ref[0, hp] * qscale, lo) for hp in hps]
        for g in range(group):
            c_ref[g] = jnp.zeros((2 * t, LANE), F32)
            acc_ref[g] = jnp.zeros((2 * t, LANE), F32)

        def step(kb, diagonal):
            rows = pl.ds(pl.multiple_of(kb * t, t), t)

            def scores(g, st):
                st["z"] = _dot_nt(qs[g], k_ref[0, hps[g], rows, :])

            def logs(g, st):
                z = st["z"]
                e = jnp.exp2(jnp.abs(z) * -math.log2(math.e))
                drop = jnp.maximum(z, 0.0) + jnp.log(1.0 + e)
                if diagonal:
                    drop = jnp.where(strict, drop, 0.0)
                st["drop"] = drop.astype(BF16)
                st["row_sum"] = _row_stat(jnp.sum(drop, axis=-1, keepdims=True))

            def suffix(g, st):
                st["drops"] = _dot(st.pop("drop"), tri)

            def weights(g, st):
                w = jnp.exp(st.pop("z") - st.pop("drops") - _lanes(c_ref[g], t))
                if diagonal:
                    w = jnp.where(strict, w, 0.0)
                st["w"] = w.astype(BF16)
                c_ref[g] += st.pop("row_sum")

            def values(g, st):
                acc_ref[g] += _dot(st.pop("w"), v_ref[0, hps[g], rows, :])

            _skewed(group, [scores, logs, suffix, weights, values])

        step(i, True)

        def body(j, c):
            step(i - 1 - j, False)
            return c

        lax.fori_loop(0, i, body, 0)
        for g, hp in enumerate(hps):
            o_ref[0, hp] = _merge_heads(acc_ref[g], lo, t).astype(o_ref.dtype)
        return carry

    lax.fori_loop(0, n_tiles // group, per_group, 0)


def _attn_c(h, *, t, group):
    bsz, _, seq, _ = h.shape
    n = C_HEADS // 2
    return pl.pallas_call(
        functools.partial(_attn_c_kernel, t=t, group=group),
        grid=(bsz, seq // t),
        in_specs=[pl.BlockSpec((1, n, t, LANE), lambda b, i: (b, 0, i, 0)),
                  pl.BlockSpec((1, n, seq, LANE), lambda b, i: (b, 1, 0, 0)),
                  pl.BlockSpec((1, n, seq, LANE), lambda b, i: (b, 2, 0, 0))],
        out_specs=pl.BlockSpec((1, n, t, LANE), lambda b, i: (b, 0, i, 0)),
        out_shape=jax.ShapeDtypeStruct((bsz, n, seq, LANE), BF16),
        scratch_shapes=[pltpu.VMEM((group, 2 * t, LANE), F32)] * 2,
        compiler_params=_cparams(("arbitrary", "arbitrary")),
        name="attn_c",
    )(h, h, h)


def _rope(x, cos_t, sin_t, first_half):
    partner = jnp.where(first_half, pltpu.roll(x, LANE - D_ROPE // 2, 1), pltpu.roll(x, D_ROPE // 2, 1))
    return x * cos_t + partner * sin_t


def _mla_proj_kernel(x_ref, wl_ref, gq_ref, gkv_ref, wuq_ref, wukv_ref, cos_ref, sin_ref,
                     qn_ref, qr_ref, kn_ref, vd_ref, kr_ref):
    xb = x_ref[0].astype(BF16)
    lat = _dot(xb, wl_ref[...])
    cq = lat[:, :D_Q_RANK]
    ckv = lat[:, D_Q_RANK:D_Q_RANK + D_KV_RANK]
    kr = lat[:, D_Q_RANK + D_KV_RANK:]
    cqn = cq * lax.rsqrt(jnp.mean(cq * cq, axis=-1, keepdims=True) + RMS_EPS) * gq_ref[...]
    ckvn = ckv * lax.rsqrt(jnp.mean(ckv * ckv, axis=-1, keepdims=True) + RMS_EPS) * gkv_ref[...]
    q = _dot(cqn.astype(BF16), wuq_ref[...])
    kv = _dot(ckvn.astype(BF16), wukv_ref[...])
    cos_t = cos_ref[...]
    sin_t = sin_ref[...]
    first_half = (lax.broadcasted_iota(jnp.int32, (1, LANE), 1) % D_ROPE) < D_ROPE // 2
    n_nope = qn_ref.shape[1]
    for c in range(n_nope):
        qn_ref[0, c] = q[:, c * LANE:(c + 1) * LANE].astype(BF16)
        kn_ref[0, c] = kv[:, c * LANE:(c + 1) * LANE].astype(BF16)
        vd_ref[0, c] = kv[:, (n_nope + c) * LANE:(n_nope + c + 1) * LANE].astype(BF16)
    for c in range(qr_ref.shape[1]):
        qr = q[:, (n_nope + c) * LANE:(n_nope + c + 1) * LANE]
        qr_ref[0, c] = _rope(qr, cos_t, sin_t, first_half).astype(BF16)
    kr_ref[0] = _rope(kr, cos_t, sin_t, first_half).astype(BF16)


def _mla_proj(x, w_lat, gq, gkv, w_uq, w_ukv, cos_t, sin_t, *, tm):
    bsz, seq, d = x.shape
    n_nope = D_HEADS * D_NOPE // LANE
    n_rope = D_HEADS * D_ROPE // LANE
    full = lambda a: pl.BlockSpec(a.shape, lambda b, s: (0,) * a.ndim)
    tile_out = lambda n: pl.BlockSpec((1, n, tm, LANE), lambda b, s: (b, 0, s, 0))
    tile_shape = lambda n: jax.ShapeDtypeStruct((bsz, n, seq, LANE), BF16)
    return pl.pallas_call(
        _mla_proj_kernel,
        grid=(bsz, seq // tm),
        in_specs=[pl.BlockSpec((1, tm, d), lambda b, s: (b, s, 0)),
                  full(w_lat), full(gq), full(gkv), full(w_uq), full(w_ukv),
                  pl.BlockSpec((tm, LANE), lambda b, s: (s, 0)),
                  pl.BlockSpec((tm, LANE), lambda b, s: (s, 0))],
        out_specs=[tile_out(n_nope), tile_out(n_rope), tile_out(n_nope), tile_out(n_nope),
                   pl.BlockSpec((1, tm, LANE), lambda b, s: (b, s, 0))],
        out_shape=[tile_shape(n_nope), tile_shape(n_rope), tile_shape(n_nope), tile_shape(n_nope),
                   jax.ShapeDtypeStruct((bsz, seq, LANE), BF16)],
        compiler_params=_cparams(("arbitrary", "arbitrary")),
        name="mla_proj",
    )(x, w_lat, gq, gkv, w_uq, w_ukv, cos_t, sin_t)


def _attn_d_kernel(qn_ref, qr_ref, kn_ref, kr_ref, vd_ref, o_ref, m_ref, l_ref, acc_ref, *, t, group):
    i = pl.program_id(1)
    n_tiles = qn_ref.shape[1]
    lane = lax.broadcasted_iota(jnp.int32, (1, LANE), 1)
    lo = lane < D_NOPE
    lane_q = lane // D_ROPE
    causal = _rel(t, t) >= 0
    scale = math.log2(math.e) / math.sqrt(D_NOPE + D_ROPE)

    def per_group(gi, carry):
        hps = [gi * group + g for g in range(group)]
        qs = []
        for g, hp in enumerate(hps):
            qn = _split_heads(qn_ref[0, hp], lo)
            qr2 = qr_ref[0, gi * (group // 2) + g // 2]
            quarter = (g % 2) * 2
            zero = jnp.zeros_like(qr2)
            qr = jnp.concatenate([jnp.where(lane_q == quarter, qr2, zero),
                                  jnp.where(lane_q == quarter + 1, qr2, zero)], axis=0)
            qs.append(jnp.concatenate([qn, qr], axis=1))
            m_ref[g] = jnp.full((2 * t, LANE), NEG, F32)
            l_ref[g] = jnp.zeros((2 * t, LANE), F32)
            acc_ref[g] = jnp.zeros((2 * t, LANE), F32)

        def step(kb, diagonal):
            rows = pl.ds(pl.multiple_of(kb * t, t), t)
            kr = kr_ref[0, rows, :]
            def score_fn(g):
                def fn():
                    s = _dot_nt(qs[g], jnp.concatenate([kn_ref[0, hps[g], rows, :], kr], axis=1)) * scale
                    return jnp.where(causal, s, NEG) if diagonal else s
                return fn

            _online_softmax_steps([score_fn(g) for g in range(group)],
                                  [functools.partial(lambda g: vd_ref[0, hps[g], rows, :], g) for g in range(group)],
                                  m_ref, l_ref, acc_ref, exp=jnp.exp2)

        step(i, True)

        def body(kb, c):
            step(kb, False)
            return c

        lax.fori_loop(0, i, body, 0)
        for g, hp in enumerate(hps):
            o_ref[0, hp] = _merge_heads(_finish(acc_ref[g], l_ref[g]), lo, t).astype(o_ref.dtype)
        return carry

    lax.fori_loop(0, n_tiles // group, per_group, 0)


def _attn_d(qn, qr, kn, kr, vd, *, t, group):
    bsz, n, seq, _ = qn.shape
    nr = qr.shape[1]
    return pl.pallas_call(
        functools.partial(_attn_d_kernel, t=t, group=group),
        grid=(bsz, seq // t),
        in_specs=[pl.BlockSpec((1, n, t, LANE), lambda b, i: (b, 0, i, 0)),
                  pl.BlockSpec((1, nr, t, LANE), lambda b, i: (b, 0, i, 0)),
                  pl.BlockSpec((1, n, seq, LANE), lambda b, i: (b, 0, 0, 0)),
                  pl.BlockSpec((1, seq, LANE), lambda b, i: (b, 0, 0)),
                  pl.BlockSpec((1, n, seq, LANE), lambda b, i: (b, 0, 0, 0))],
        out_specs=pl.BlockSpec((1, n, t, LANE), lambda b, i: (b, 0, i, 0)),
        out_shape=jax.ShapeDtypeStruct((bsz, n, seq, LANE), BF16),
        scratch_shapes=[pltpu.VMEM((group, 2 * t, LANE), F32)] * 3,
        compiler_params=_cparams(("arbitrary", "arbitrary")),
        name="attn_d",
    )(qn, qr, kn, kr, vd)


def _out_ln_kernel(x_ref, ya_ref, yb_ref, wa_ref, wb_ref, g_ref, b_ref, o_ref, *, alpha):
    ya = jnp.concatenate([ya_ref[0, c] for c in range(ya_ref.shape[1])], axis=1)
    yb = jnp.concatenate([yb_ref[0, c] for c in range(yb_ref.shape[1])], axis=1)
    mixed = _dot(ya, wa_ref[...]) + _dot(yb, wb_ref[...])
    o_ref[0] = _layer_norm(alpha * x_ref[0] + mixed, g_ref[...], b_ref[...])


def _out_ln(x, ya, yb, w, g, b, *, alpha, tm):
    bsz, seq, d = x.shape
    ka, kb = ya.shape[1] * LANE, yb.shape[1] * LANE
    assert w.shape[0] == ka + kb and ka % kb == 0
    full = lambda a: pl.BlockSpec(a.shape, lambda bb, s: (0,) * a.ndim)
    return pl.pallas_call(
        functools.partial(_out_ln_kernel, alpha=alpha),
        grid=(bsz, seq // tm),
        in_specs=[pl.BlockSpec((1, tm, d), lambda bb, s: (bb, s, 0)),
                  pl.BlockSpec((1, ya.shape[1], tm, LANE), lambda bb, s: (bb, 0, s, 0)),
                  pl.BlockSpec((1, yb.shape[1], tm, LANE), lambda bb, s: (bb, 0, s, 0)),
                  pl.BlockSpec((ka, d), lambda bb, s: (0, 0)),
                  pl.BlockSpec((kb, d), lambda bb, s: (ka // kb, 0)),
                  full(g), full(b)],
        out_specs=pl.BlockSpec((1, tm, d), lambda bb, s: (bb, s, 0)),
        out_shape=jax.ShapeDtypeStruct((bsz, seq, d), F32),
        compiler_params=_cparams(("arbitrary", "arbitrary")),
        name="out_ln",
    )(x, ya, yb, w, w, g, b)


def _mlp_ln_kernel(x_ref, w1_ref, w2_ref, g_ref, b_ref, o_ref, xb_ref, *, alpha):
    f = pl.program_id(1)

    @pl.when(f == 0)
    def _():
        xb_ref[...] = x_ref[...].astype(BF16)
        o_ref[...] = jnp.zeros_like(o_ref)

    hid = jnp.maximum(_dot(xb_ref[...], w1_ref[...]), 0.0)
    o_ref[...] += _dot((hid * hid).astype(BF16), w2_ref[...])

    @pl.when(f == pl.num_programs(1) - 1)
    def _():
        o_ref[...] = _layer_norm(alpha * x_ref[...] + o_ref[...], g_ref[...], b_ref[...])


def _mlp_ln(x, w1, w2, g, b, *, alpha, tm, tf):
    m, d = x.shape
    ff = w1.shape[1]
    return pl.pallas_call(
        functools.partial(_mlp_ln_kernel, alpha=alpha),
        grid=(m // tm, ff // tf),
        in_specs=[pl.BlockSpec((tm, d), lambda i, f: (i, 0)),
                  pl.BlockSpec((d, tf), lambda i, f: (0, f)),
                  pl.BlockSpec((tf, d), lambda i, f: (f, 0)),
                  pl.BlockSpec((1, d), lambda i, f: (0, 0)),
                  pl.BlockSpec((1, d), lambda i, f: (0, 0))],
        out_specs=pl.BlockSpec((tm, d), lambda i, f: (i, 0)),
        out_shape=jax.ShapeDtypeStruct((m, d), F32),
        scratch_shapes=[pltpu.VMEM((tm, d), BF16)],
        compiler_params=_cparams(("arbitrary", "arbitrary")),
        name="mlp_ln",
    )(x, w1, w2, g, b)


def _even_w_in_layout(w):
    qa = w[:, :A_Q_W]
    ka = w[:, A_Q_W:A_Q_W + A_KV_W]
    va = w[:, A_Q_W + A_KV_W:A_Q_W + 2 * A_KV_W]
    hb = w[:, A_Q_W + 2 * A_KV_W:]
    dup = lambda kv: [kv[:, h * HEAD_DIM:(h + 1) * HEAD_DIM] for h in range(A_KV_HEADS) for _ in range(2)]
    return jnp.concatenate([qa, hb] + dup(ka) + dup(va), axis=1).astype(BF16)


def _rope_tables(seq):
    inv_freq = ROPE_BASE ** (-jnp.arange(0, D_ROPE, 2, dtype=F32) / D_ROPE)
    ang = jnp.arange(seq, dtype=F32)[:, None] * inv_freq[None, :]
    cos, sin = jnp.cos(ang), jnp.sin(ang)
    reps = LANE // D_ROPE
    cos_t = jnp.tile(jnp.concatenate([cos, cos], axis=1), (1, reps))
    sin_t = jnp.tile(jnp.concatenate([-sin, sin], axis=1), (1, reps))
    return cos_t, sin_t


def _even_layer(x, w_in, sinks, w_out, g, b, alpha):
    h = _proj(x, _even_w_in_layout(w_in), tm=512, n_split=2)
    n_qa = A_Q_W // LANE
    n_b = 3 * len(B_PATTERNS) * B_W // LANE
    oa = _attn_a(h, sinks, q_tile0=0, k_tile0=n_qa + n_b, v_tile0=n_qa + n_b + A_KV_HEADS)
    ob = _attn_b(h, tile0=n_qa)
    return _out_ln(x, oa, ob, w_out.astype(BF16), g, b, alpha=alpha, tm=512)


def _odd_layer(x, w_in, gq, gkv, w_uq, w_ukv, w_out, g, b, alpha):
    seq = x.shape[1]
    hc = _proj(x, w_in[:, :3 * C_W].astype(BF16), tm=512, n_split=1)
    oc = _attn_c(hc, t=256, group=8)
    w_lat = w_in[:, 3 * C_W:]
    kr_w = w_lat[:, D_Q_RANK + D_KV_RANK:]
    w_lat = jnp.concatenate([w_lat[:, :D_Q_RANK + D_KV_RANK]] + [kr_w] * (LANE // D_ROPE), axis=1).astype(BF16)
    uq = w_uq.reshape(D_Q_RANK, D_HEADS, D_NOPE + D_ROPE)
    uq = jnp.concatenate([uq[:, :, :D_NOPE].reshape(D_Q_RANK, -1), uq[:, :, D_NOPE:].reshape(D_Q_RANK, -1)], axis=1)
    ukv = w_ukv.reshape(D_KV_RANK, D_HEADS, D_NOPE + D_V)
    ukv = jnp.concatenate([ukv[:, :, :D_NOPE].reshape(D_KV_RANK, -1), ukv[:, :, D_NOPE:].reshape(D_KV_RANK, -1)], axis=1)
    cos_t, sin_t = _rope_tables(seq)
    qn, qr, kn, vd, kr = _mla_proj(x, w_lat, gq.reshape(1, -1), gkv.reshape(1, -1),
                                   uq.astype(BF16), ukv.astype(BF16), cos_t, sin_t, tm=512)
    od = _attn_d(qn, qr, kn, kr, vd, t=256, group=8)
    return _out_ln(x, oc, od, w_out.astype(BF16), g, b, alpha=alpha, tm=512)


def kernel(x, even_w_in, even_sinks, even_w_out, odd_w_in, odd_q_norm_g, odd_kv_norm_g, odd_w_uq, odd_w_ukv,
           odd_w_out, ln1_g, ln1_b, mlp_w1, mlp_w2, ln2_g, ln2_b):
    bsz, seq, d = x.shape
    depth = ln1_g.shape[0]
    alpha = (2 * depth) ** 0.25
    for layer in range(depth):
        j = layer // 2
        g1, b1 = ln1_g[layer].reshape(1, d), ln1_b[layer].reshape(1, d)
        if layer % 2 == 0:
            x = _even_layer(x, even_w_in[j], even_sinks[j], even_w_out[j], g1, b1, alpha)
        else:
            x = _odd_layer(x, odd_w_in[j], odd_q_norm_g[j], odd_kv_norm_g[j], odd_w_uq[j], odd_w_ukv[j],
                           odd_w_out[j], g1, b1, alpha)
        x = _mlp_ln(x.reshape(bsz * seq, d), mlp_w1[layer].astype(BF16), mlp_w2[layer].astype(BF16),
                    ln2_g[layer].reshape(1, d), ln2_b[layer].reshape(1, d), alpha=alpha, tm=1024, tf=512
                    ).reshape(bsz, seq, d)
    return x
```

```python
import functools
import math

import jax
import jax.numpy as jnp
from jax import lax
from jax.experimental import pallas as pl
from jax.experimental.pallas import tpu as pltpu

LANE = 128
V7X_VMEM_BYTES = 64 * 1024 * 1024
VMEM_LIMIT = V7X_VMEM_BYTES - 8 * 1024 * 1024

HEAD_DIM = 64
A_Q_HEADS = 16
A_KV_HEADS = 2
A_WINDOW = 128
B_HEADS = 8
B_PATTERNS = ((128, 1), (512, 4), (2048, 16))
C_HEADS = 16
D_HEADS = 16
D_Q_RANK = 512
D_KV_RANK = 256
D_NOPE = 64
D_ROPE = 32
D_V = 64
ROPE_BASE = 10000.0
LN_EPS = 1e-5
RMS_EPS = 1e-6
BLOCK = 128
NEG = -1e30

A_Q_W = A_Q_HEADS * HEAD_DIM
A_KV_W = A_KV_HEADS * HEAD_DIM
B_W = B_HEADS * HEAD_DIM
C_W = C_HEADS * HEAD_DIM

BF16 = jnp.bfloat16
F32 = jnp.float32


def _cparams(sem):
    return pltpu.CompilerParams(dimension_semantics=sem, vmem_limit_bytes=VMEM_LIMIT)


def _dot(a, b):
    return jnp.dot(a, b, preferred_element_type=F32)


def _dot_nt(a, b):
    return lax.dot_general(a, b, (((1,), (1,)), ((), ())), preferred_element_type=F32)


def _layer_norm(r, g, b):
    mu = jnp.mean(r, axis=-1, keepdims=True)
    rc = r - mu
    var = jnp.mean(rc * rc, axis=-1, keepdims=True)
    return rc * lax.rsqrt(var + LN_EPS) * g + b


def _proj_kernel(x_ref, w_ref, o_ref):
    xb = x_ref[0].astype(BF16)
    acc = _dot(xb, w_ref[...])
    for c in range(o_ref.shape[1]):
        o_ref[0, c] = acc[:, c * LANE:(c + 1) * LANE].astype(o_ref.dtype)


def _proj(x, w, *, tm, n_split):
    bsz, seq, d = x.shape
    tiles = w.shape[1] // LANE
    tps = tiles // n_split
    return pl.pallas_call(
        _proj_kernel,
        grid=(n_split, bsz, seq // tm),
        in_specs=[pl.BlockSpec((1, tm, d), lambda n, b, s: (b, s, 0)),
                  pl.BlockSpec((d, tps * LANE), lambda n, b, s: (0, n))],
        out_specs=pl.BlockSpec((1, tps, tm, LANE), lambda n, b, s: (b, n, s, 0)),
        out_shape=jax.ShapeDtypeStruct((bsz, tiles, seq, LANE), BF16),
        compiler_params=_cparams(("arbitrary", "arbitrary", "arbitrary")),
        name="proj",
    )(x, w)


def _split_heads(q2, lo):
    zero = jnp.zeros_like(q2)
    return jnp.concatenate([jnp.where(lo, q2, zero), jnp.where(lo, zero, q2)], axis=0)


def _merge_heads(o, lo, t):
    return jnp.where(lo, o[:t], o[t:])


def _lanes(x, n):
    return x[:, :n] if n <= LANE else jnp.concatenate([x] * (n // LANE), axis=1)


def _row_stat(x):
    return jnp.broadcast_to(x, (x.shape[0], LANE))


def _fold_lanes(p):
    out = p[:, :LANE]
    for c in range(1, p.shape[1] // LANE):
        out = out + p[:, c * LANE:(c + 1) * LANE]
    return out


def _finish(acc, l_part):
    return acc / _row_stat(jnp.sum(l_part, axis=-1, keepdims=True))


def _skewed(n_chains, stages):
    state = [dict() for _ in range(n_chains)]
    for step in range(n_chains + len(stages) - 1):
        for g in range(n_chains):
            s = step - g
            if 0 <= s < len(stages):
                stages[s](g, state[g])


def _rel(t, n):
    row = lax.broadcasted_iota(jnp.int32, (2 * t, n), 0)
    col = lax.broadcasted_iota(jnp.int32, (2 * t, n), 1)
    return jnp.where(row >= t, row - t, row) - col


def _per_head(t, n, even, odd):
    row = lax.broadcasted_iota(jnp.int32, (2 * t, n), 0)
    return jnp.where(row >= t, jnp.asarray(odd, F32), jnp.asarray(even, F32))


def _online_softmax_steps(ss, vs, m_ref, l_ref, acc_ref, exp=jnp.exp):
    def scores(g, st):
        st["s"] = ss[g]()

    def probabilities(g, st):
        s = st.pop("s")
        m_prev = m_ref[g]
        m_new = jnp.maximum(m_prev, _row_stat(jnp.max(s, axis=-1, keepdims=True)))
        alpha = exp(m_prev - m_new)
        p = exp(s - _lanes(m_new, s.shape[1]))
        l_ref[g] = alpha * l_ref[g] + _fold_lanes(p)
        m_ref[g] = m_new
        st["p"] = p.astype(BF16)
        st["alpha"] = alpha

    def values(g, st):
        acc_ref[g] = st.pop("alpha") * acc_ref[g] + _dot(st.pop("p"), vs[g]())

    state = [dict() for _ in ss]
    for stage in (scores, probabilities, values):
        for g, st in enumerate(state):
            stage(g, st)


def _window_start(i, back, n_blocks, seq_blocks):
    return jnp.clip(i - back, 0, seq_blocks - n_blocks)


def _attn_a_kernel(sinks_ref, q_ref, k_ref, v_ref, o_ref, *, t):
    i = pl.program_id(1)
    n_tiles = q_ref.shape[1]
    seq_blocks = k_ref.shape[2] // t
    win_blocks = min(2, seq_blocks)
    n = win_blocks * t
    heads_per_kv = A_Q_HEADS // A_KV_HEADS
    lo = lax.broadcasted_iota(jnp.int32, (1, LANE), 1) < HEAD_DIM
    start = _window_start(i, 1, win_blocks, seq_blocks)
    rows = pl.ds(pl.multiple_of(start * t, t), n)
    dist = _rel(t, n) + (i - start) * t
    distf = dist.astype(F32)
    valid = (dist >= 0) & (dist < A_WINDOW)
    hi_half = lax.broadcasted_iota(jnp.int32, (2 * t, LANE), 0) >= t
    kv = [(k_ref[0, kh, rows, :], v_ref[0, kh, rows, :]) for kh in range(A_KV_HEADS)]

    def scores(hp, st):
        q = _split_heads(q_ref[0, hp] * jnp.asarray(1.0 / math.sqrt(HEAD_DIM), BF16), lo)
        slope = _per_head(t, n, 2.0 ** (-(2 * hp + 1) * 8.0 / A_Q_HEADS), 2.0 ** (-(2 * hp + 2) * 8.0 / A_Q_HEADS))
        st["s"] = jnp.where(valid, _dot_nt(q, kv[(2 * hp) // heads_per_kv][0]) - slope * distf, NEG)

    def probabilities(hp, st):
        s = st.pop("s")
        sink = jnp.where(hi_half, sinks_ref[2 * hp + 1], sinks_ref[2 * hp])
        m = jnp.maximum(_row_stat(jnp.max(s, axis=-1, keepdims=True)), sink)
        p = jnp.exp(s - _lanes(m, n))
        st["denom"] = _row_stat(jnp.sum(p, axis=-1, keepdims=True)) + jnp.exp(sink - m)
        st["p"] = p.astype(BF16)

    def values(hp, st):
        o = _dot(st.pop("p"), kv[(2 * hp) // heads_per_kv][1]) / st.pop("denom")
        o_ref[0, hp] = _merge_heads(o, lo, t).astype(o_ref.dtype)

    state = [dict() for _ in range(n_tiles)]
    for stage in (scores, probabilities, values):
        for hp in range(n_tiles):
            stage(hp, state[hp])


def _attn_a(h, sinks, *, q_tile0, k_tile0, v_tile0):
    bsz, _, seq, _ = h.shape
    t = A_WINDOW
    nq = A_Q_HEADS // 2
    return pl.pallas_call(
        functools.partial(_attn_a_kernel, t=t),
        grid=(bsz, seq // t),
        in_specs=[pl.BlockSpec(memory_space=pltpu.SMEM),
                  pl.BlockSpec((1, nq, t, LANE), lambda b, i: (b, q_tile0 // nq, i, 0)),
                  pl.BlockSpec((1, A_KV_HEADS, seq, LANE), lambda b, i: (b, k_tile0 // A_KV_HEADS, 0, 0)),
                  pl.BlockSpec((1, A_KV_HEADS, seq, LANE), lambda b, i: (b, v_tile0 // A_KV_HEADS, 0, 0))],
        out_specs=pl.BlockSpec((1, nq, t, LANE), lambda b, i: (b, 0, i, 0)),
        out_shape=jax.ShapeDtypeStruct((bsz, nq, seq, LANE), BF16),
        compiler_params=_cparams(("arbitrary", "arbitrary")),
        name="attn_a",
    )(sinks, h, h, h)


def _b_slopes(hp):
    return 2.0 ** (-(2 * hp + 1) * 8.0 / B_HEADS), 2.0 ** (-(2 * hp + 2) * 8.0 / B_HEADS)


def _attn_b_dilated_kernel(q_ref, k_ref, v_ref, o_ref, lse_ref, qf_ref, kf_ref, vf_ref, *, window, dil):
    hp = pl.program_id(1)
    seq = q_ref.shape[2]
    n = seq // dil
    lo = lax.broadcasted_iota(jnp.int32, (1, LANE), 1) < HEAD_DIM
    qf_ref[...] = q_ref[0, 0].astype(F32)
    kf_ref[...] = k_ref[0, 0].astype(F32)
    vf_ref[...] = v_ref[0, 0].astype(F32)
    tb = min(n, BLOCK)
    n_blocks = n // tb
    reach = window // dil
    win_blocks = min(reach // tb + 1, n_blocks)
    nk = win_blocks * tb
    head = (2 * hp + (lax.broadcasted_iota(jnp.int32, (2 * tb, nk), 0) >= tb).astype(jnp.int32)).astype(F32)
    slope = jnp.exp2(-(head + 1.0) * (8.0 / B_HEADS))
    qscale = jnp.asarray(1.0 / math.sqrt(HEAD_DIM), BF16)
    biases = {}
    chains = []
    for j in range(n_blocks):
        start = min(max(j - reach // tb, 0), n_blocks - win_blocks)
        if j - start not in biases:
            dist = _rel(tb, nk) + (j - start) * tb
            biases[j - start] = jnp.where((dist >= 0) & (dist <= reach), -slope * (dist * dil).astype(F32), NEG)
        for r in range(dil):
            chains.append((pl.ds(r + dil * j * tb, tb, stride=dil), pl.ds(r + dil * start * tb, nk, stride=dil),
                           biases[j - start]))

    def scores(c, st):
        q_rows, k_rows, bias = chains[c]
        q = _split_heads(qf_ref[q_rows, :].astype(BF16) * qscale, lo)
        st["s"] = _dot_nt(q, kf_ref[k_rows, :].astype(BF16)) + bias

    def probabilities(c, st):
        s = st.pop("s")
        m = _row_stat(jnp.max(s, axis=-1, keepdims=True))
        p = jnp.exp(s - _lanes(m, nk))
        st["l"] = _row_stat(jnp.sum(p, axis=-1, keepdims=True))
        st["m"] = m
        st["p"] = p.astype(BF16)

    def values(c, st):
        q_rows, k_rows, _ = chains[c]
        l = st.pop("l")
        o = _dot(st.pop("p"), vf_ref[k_rows, :].astype(BF16)) / l
        o_ref[0, 0, q_rows, :] = _merge_heads(o, lo, tb)
        lse_ref[0, 0, q_rows, :] = _merge_heads(st.pop("m") + jnp.log(l), lo, tb)

    state = [dict() for _ in chains]
    for stage in (scores, probabilities, values):
        for c in range(len(chains)):
            stage(c, state[c])


def _attn_b_dilated(h, *, tile0, window, dil):
    bsz, _, seq, _ = h.shape
    n = B_HEADS // 2
    spec = lambda off: pl.BlockSpec((1, 1, seq, LANE), lambda b, hp: (b, tile0 + off + hp, 0, 0))
    out_spec = pl.BlockSpec((1, 1, seq, LANE), lambda b, hp: (b, hp, 0, 0))
    out_shape = jax.ShapeDtypeStruct((bsz, n, seq, LANE), F32)
    return pl.pallas_call(
        functools.partial(_attn_b_dilated_kernel, window=window, dil=dil),
        grid=(bsz, n),
        in_specs=[spec(0), spec(n), spec(2 * n)],
        out_specs=[out_spec, out_spec],
        out_shape=[out_shape, out_shape],
        scratch_shapes=[pltpu.VMEM((seq, LANE), F32)] * 3,
        compiler_params=_cparams(("arbitrary", "arbitrary")),
        name="attn_b_dilated",
    )(h, h, h)


def _attn_b_kernel(*refs, t, patterns, n_extra):
    groups = [refs[3 * g:3 * g + 3] for g in range(len(patterns))]
    extras = [refs[3 * len(patterns) + 2 * e:3 * len(patterns) + 2 * e + 2] for e in range(n_extra)]
    o_ref = refs[-1]
    i = pl.program_id(1)
    n_tiles = o_ref.shape[1]
    seq_blocks = groups[0][1].shape[2] // t
    lo = lax.broadcasted_iota(jnp.int32, (1, LANE), 1) < HEAD_DIM
    qscale = jnp.asarray(1.0 / math.sqrt(HEAD_DIM), BF16)
    slopes = [_b_slopes(hp) for hp in range(n_tiles)]

    bounded = []
    for (q_ref, k_ref, v_ref), (window, dil) in zip(groups, patterns):
        back = window // t
        win_blocks = min(back + 1, seq_blocks)
        n = win_blocks * t
        start = _window_start(i, back, win_blocks, seq_blocks)
        dist = _rel(t, n) + (i - start) * t
        valid = (dist >= 0) & (dist <= window) & ((dist & (dil - 1)) == 0)
        bounded.append((q_ref, k_ref, v_ref, pl.ds(pl.multiple_of(start * t, t), n), n, dist.astype(F32), valid))

    def scores(hp, st):
        st["s"] = []
        for q_ref, k_ref, v_ref, rows, n, distf, valid in bounded:
            q = _split_heads(q_ref[0, hp] * qscale, lo)
            s = _dot_nt(q, k_ref[0, hp, rows, :]) - _per_head(t, n, *slopes[hp]) * distf
            st["s"].append(jnp.where(valid, s, NEG))

    def probabilities(hp, st):
        m = jnp.full((2 * t, LANE), NEG, F32)
        for s in st["s"]:
            m = jnp.maximum(m, _row_stat(jnp.max(s, axis=-1, keepdims=True)))
        l = jnp.zeros((2 * t, LANE), F32)
        st["p"] = []
        for s in st.pop("s"):
            p = jnp.exp(s - _lanes(m, s.shape[1]))
            l = l + _fold_lanes(p)
            st["p"].append(p.astype(BF16))
        st["m"] = m
        st["l"] = _row_stat(jnp.sum(l, axis=-1, keepdims=True))

    def values(hp, st):
        acc = jnp.zeros((2 * t, LANE), F32)
        for p, (_, _, v_ref, rows, _, _, _) in zip(st.pop("p"), bounded):
            acc = acc + _dot(p, v_ref[0, hp, rows, :])
        m, l, acc = (_merge_heads(x, lo, t) for x in (st.pop("m"), st.pop("l"), acc))
        top = m
        for _, lse_ref in extras:
            top = jnp.maximum(top, lse_ref[0, hp])
        scale = jnp.exp(m - top)
        numer, denom = scale * acc, scale * l
        for oe_ref, lse_ref in extras:
            weight = jnp.exp(lse_ref[0, hp] - top)
            numer = numer + weight * oe_ref[0, hp]
            denom = denom + weight
        o_ref[0, hp] = (numer / denom).astype(o_ref.dtype)

    state = [dict() for _ in range(n_tiles)]
    for stage in (scores, probabilities, values):
        for hp in range(n_tiles):
            stage(hp, state[hp])


def _attn_b(h, *, tile0):
    bsz, _, seq, _ = h.shape
    t = BLOCK
    n = B_HEADS // 2
    in_specs, bounded, extras = [], [], []
    for g, (window, dil) in enumerate(B_PATTERNS):
        base = tile0 + 3 * n * g
        if dil > 1:
            extras.extend(_attn_b_dilated(h, tile0=base, window=window, dil=dil))
            continue
        bounded.append((window, dil))
        in_specs.append(pl.BlockSpec((1, n, t, LANE), lambda b, i, base=base: (b, base // n, i, 0)))
        in_specs.append(pl.BlockSpec((1, n, seq, LANE), lambda b, i, base=base: (b, base // n + 1, 0, 0)))
        in_specs.append(pl.BlockSpec((1, n, seq, LANE), lambda b, i, base=base: (b, base // n + 2, 0, 0)))
    assert bounded, "at least one undilated group is expected"
    in_specs += [pl.BlockSpec((1, n, t, LANE), lambda b, i: (b, 0, i, 0))] * len(extras)
    return pl.pallas_call(
        functools.partial(_attn_b_kernel, t=t, patterns=tuple(bounded), n_extra=len(extras) // 2),
        grid=(bsz, seq // t),
        in_specs=in_specs,
        out_specs=pl.BlockSpec((1, n, t, LANE), lambda b, i: (b, 0, i, 0)),
        out_shape=jax.ShapeDtypeStruct((bsz, n, seq, LANE), BF16),
        compiler_params=_cparams(("arbitrary", "arbitrary")),
        name="attn_b",
    )(*([h] * (3 * len(bounded)) + extras))


def _attn_c_kernel(q_ref, k_ref, v_ref, o_ref, c_ref, acc_ref, *, t, group):
    i = pl.program_id(1)
    n_tiles = q_ref.shape[1]
    lo = lax.broadcasted_iota(jnp.int32, (1, LANE), 1) < HEAD_DIM
    strict = _rel(t, t) > 0
    jj = lax.broadcasted_iota(jnp.int32, (t, t), 0)
    ss = lax.broadcasted_iota(jnp.int32, (t, t), 1)
    tri = jnp.where(jj >= ss, 1.0, 0.0).astype(BF16)
    qscale = jnp.asarray(1.0 / math.sqrt(HEAD_DIM), BF16)

    def per_group(gi, carry):
        hps = [gi * group + g for g in range(group)]
        qs = [_split_heads(q_ref[0, hp] * qscale, lo) for hp in hps]
        for g in range(group):
            c_ref[g] = jnp.zeros((2 * t, LANE), F32)
            acc_ref[g] = jnp.zeros((2 * t, LANE), F32)

        def step(kb, diagonal):
            rows = pl.ds(pl.multiple_of(kb * t, t), t)

            def scores(g, st):
                st["z"] = _dot_nt(qs[g], k_ref[0, hps[g], rows, :])

            def logs(g, st):
                z = st["z"]
                e = jnp.exp2(jnp.abs(z) * -math.log2(math.e))
                drop = jnp.maximum(z, 0.0) + jnp.log(1.0 + e)
                if diagonal:
                    drop = jnp.where(strict, drop, 0.0)
                st["drop"] = drop.astype(BF16)
                st["row_sum"] = _row_stat(jnp.sum(drop, axis=-1, keepdims=True))

            def suffix(g, st):
                st["drops"] = _dot(st.pop("drop"), tri)

            def weights(g, st):
                w = jnp.exp(st.pop("z") - st.pop("drops") - _lanes(c_ref[g], t))
                if diagonal:
                    w = jnp.where(strict, w, 0.0)
                st["w"] = w.astype(BF16)
                c_ref[g] += st.pop("row_sum")

            def values(g, st):
                acc_ref[g] += _dot(st.pop("w"), v_ref[0, hps[g], rows, :])

            _skewed(group, [scores, logs, suffix, weights, values])

        step(i, True)

        def body(j, c):
            step(i - 1 - j, False)
            return c

        lax.fori_loop(0, i, body, 0)
        for g, hp in enumerate(hps):
            o_ref[0, hp] = _merge_heads(acc_ref[g], lo, t).astype(o_ref.dtype)
        return carry

    lax.fori_loop(0, n_tiles // group, per_group, 0)


def _attn_c(h, *, t, group):
    bsz, _, seq, _ = h.shape
    n = C_HEADS // 2
    return pl.pallas_call(
        functools.partial(_attn_c_kernel, t=t, group=group),
        grid=(bsz, seq // t),
        in_specs=[pl.BlockSpec((1, n, t, LANE), lambda b, i: (b, 0, i, 0)),
                  pl.BlockSpec((1, n, seq, LANE), lambda b, i: (b, 1, 0, 0)),
                  pl.BlockSpec((1, n, seq, LANE), lambda b, i: (b, 2, 0, 0))],
        out_specs=pl.BlockSpec((1, n, t, LANE), lambda b, i: (b, 0, i, 0)),
        out_shape=jax.ShapeDtypeStruct((bsz, n, seq, LANE), BF16),
        scratch_shapes=[pltpu.VMEM((group, 2 * t, LANE), F32)] * 2,
        compiler_params=_cparams(("arbitrary", "arbitrary")),
        name="attn_c",
    )(h, h, h)


def _rope(x, cos_t, sin_t, first_half):
    partner = jnp.where(first_half, pltpu.roll(x, LANE - D_ROPE // 2, 1), pltpu.roll(x, D_ROPE // 2, 1))
    return x * cos_t + partner * sin_t


def _mla_proj_kernel(x_ref, wl_ref, gq_ref, gkv_ref, wuq_ref, wukv_ref, cos_ref, sin_ref,
                     qn_ref, qr_ref, kn_ref, vd_ref, kr_ref):
    xb = x_ref[0].astype(BF16)
    lat = _dot(xb, wl_ref[...])
    cq = lat[:, :D_Q_RANK]
    ckv = lat[:, D_Q_RANK:D_Q_RANK + D_KV_RANK]
    kr = lat[:, D_Q_RANK + D_KV_RANK:]
    cqn = cq * lax.rsqrt(jnp.mean(cq * cq, axis=-1, keepdims=True) + RMS_EPS) * gq_ref[...]
    ckvn = ckv * lax.rsqrt(jnp.mean(ckv * ckv, axis=-1, keepdims=True) + RMS_EPS) * gkv_ref[...]
    q = _dot(cqn.astype(BF16), wuq_ref[...])
    kv = _dot(ckvn.astype(BF16), wukv_ref[...])
    cos_t = cos_ref[...]
    sin_t = sin_ref[...]
    first_half = (lax.broadcasted_iota(jnp.int32, (1, LANE), 1) % D_ROPE) < D_ROPE // 2
    n_nope = qn_ref.shape[1]
    for c in range(n_nope):
        qn_ref[0, c] = q[:, c * LANE:(c + 1) * LANE].astype(BF16)
        kn_ref[0, c] = kv[:, c * LANE:(c + 1) * LANE].astype(BF16)
        vd_ref[0, c] = kv[:, (n_nope + c) * LANE:(n_nope + c + 1) * LANE].astype(BF16)
    for c in range(qr_ref.shape[1]):
        qr = q[:, (n_nope + c) * LANE:(n_nope + c + 1) * LANE]
        qr_ref[0, c] = _rope(qr, cos_t, sin_t, first_half).astype(BF16)
    kr_ref[0] = _rope(kr, cos_t, sin_t, first_half).astype(BF16)


def _mla_proj(x, w_lat, gq, gkv, w_uq, w_ukv, cos_t, sin_t, *, tm):
    bsz, seq, d = x.shape
    n_nope = D_HEADS * D_NOPE // LANE
    n_rope = D_HEADS * D_ROPE // LANE
    full = lambda a: pl.BlockSpec(a.shape, lambda b, s: (0,) * a.ndim)
    tile_out = lambda n: pl.BlockSpec((1, n, tm, LANE), lambda b, s: (b, 0, s, 0))
    tile_shape = lambda n: jax.ShapeDtypeStruct((bsz, n, seq, LANE), BF16)
    return pl.pallas_call(
        _mla_proj_kernel,
        grid=(bsz, seq // tm),
        in_specs=[pl.BlockSpec((1, tm, d), lambda b, s: (b, s, 0)),
                  full(w_lat), full(gq), full(gkv), full(w_uq), full(w_ukv),
                  pl.BlockSpec((tm, LANE), lambda b, s: (s, 0)),
                  pl.BlockSpec((tm, LANE), lambda b, s: (s, 0))],
        out_specs=[tile_out(n_nope), tile_out(n_rope), tile_out(n_nope), tile_out(n_nope),
                   pl.BlockSpec((1, tm, LANE), lambda b, s: (b, s, 0))],
        out_shape=[tile_shape(n_nope), tile_shape(n_rope), tile_shape(n_nope), tile_shape(n_nope),
                   jax.ShapeDtypeStruct((bsz, seq, LANE), BF16)],
        compiler_params=_cparams(("arbitrary", "arbitrary")),
        name="mla_proj",
    )(x, w_lat, gq, gkv, w_uq, w_ukv, cos_t, sin_t)


def _attn_d_kernel(qn_ref, qr_ref, kn_ref, kr_ref, vd_ref, o_ref, m_ref, l_ref, acc_ref, *, t, group):
    i = pl.program_id(1)
    n_tiles = qn_ref.shape[1]
    lane = lax.broadcasted_iota(jnp.int32, (1, LANE), 1)
    lo = lane < D_NOPE
    lane_q = lane // D_ROPE
    causal = _rel(t, t) >= 0
    scale = math.log2(math.e) / math.sqrt(D_NOPE + D_ROPE)

    def per_group(gi, carry):
        hps = [gi * group + g for g in range(group)]
        qs = []
        for g, hp in enumerate(hps):
            qn = _split_heads(qn_ref[0, hp], lo)
            qr2 = qr_ref[0, gi * (group // 2) + g // 2]
            quarter = (g % 2) * 2
            zero = jnp.zeros_like(qr2)
            qr = jnp.concatenate([jnp.where(lane_q == quarter, qr2, zero),
                                  jnp.where(lane_q == quarter + 1, qr2, zero)], axis=0)
            qs.append(jnp.concatenate([qn, qr], axis=1))
            m_ref[g] = jnp.full((2 * t, LANE), NEG, F32)
            l_ref[g] = jnp.zeros((2 * t, LANE), F32)
            acc_ref[g] = jnp.zeros((2 * t, LANE), F32)

        def step(kb, diagonal):
            rows = pl.ds(pl.multiple_of(kb * t, t), t)
            kr = kr_ref[0, rows, :]
            def score_fn(g):
                def fn():
                    s = _dot_nt(qs[g], jnp.concatenate([kn_ref[0, hps[g], rows, :], kr], axis=1)) * scale
                    return jnp.where(causal, s, NEG) if diagonal else s
                return fn

            _online_softmax_steps([score_fn(g) for g in range(group)],
                                  [functools.partial(lambda g: vd_ref[0, hps[g], rows, :], g) for g in range(group)],
                                  m_ref, l_ref, acc_ref, exp=jnp.exp2)

        step(i, True)

        def body(kb, c):
            step(kb, False)
            return c

        lax.fori_loop(0, i, body, 0)
        for g, hp in enumerate(hps):
            o_ref[0, hp] = _merge_heads(_finish(acc_ref[g], l_ref[g]), lo, t).astype(o_ref.dtype)
        return carry

    lax.fori_loop(0, n_tiles // group, per_group, 0)


def _attn_d(qn, qr, kn, kr, vd, *, t, group):
    bsz, n, seq, _ = qn.shape
    nr = qr.shape[1]
    return pl.pallas_call(
        functools.partial(_attn_d_kernel, t=t, group=group),
        grid=(bsz, seq // t),
        in_specs=[pl.BlockSpec((1, n, t, LANE), lambda b, i: (b, 0, i, 0)),
                  pl.BlockSpec((1, nr, t, LANE), lambda b, i: (b, 0, i, 0)),
                  pl.BlockSpec((1, n, seq, LANE), lambda b, i: (b, 0, 0, 0)),
                  pl.BlockSpec((1, seq, LANE), lambda b, i: (b, 0, 0)),
                  pl.BlockSpec((1, n, seq, LANE), lambda b, i: (b, 0, 0, 0))],
        out_specs=pl.BlockSpec((1, n, t, LANE), lambda b, i: (b, 0, i, 0)),
        out_shape=jax.ShapeDtypeStruct((bsz, n, seq, LANE), BF16),
        scratch_shapes=[pltpu.VMEM((group, 2 * t, LANE), F32)] * 3,
        compiler_params=_cparams(("arbitrary", "arbitrary")),
        name="attn_d",
    )(qn, qr, kn, kr, vd)


def _out_ln_kernel(x_ref, ya_ref, yb_ref, wa_ref, wb_ref, g_ref, b_ref, o_ref, *, alpha):
    ya = jnp.concatenate([ya_ref[0, c] for c in range(ya_ref.shape[1])], axis=1)
    yb = jnp.concatenate([yb_ref[0, c] for c in range(yb_ref.shape[1])], axis=1)
    mixed = _dot(ya, wa_ref[...]) + _dot(yb, wb_ref[...])
    o_ref[0] = _layer_norm(alpha * x_ref[0] + mixed, g_ref[...], b_ref[...])


def _out_ln(x, ya, yb, w, g, b, *, alpha, tm):
    bsz, seq, d = x.shape
    ka, kb = ya.shape[1] * LANE, yb.shape[1] * LANE
    assert w.shape[0] == ka + kb and ka % kb == 0
    full = lambda a: pl.BlockSpec(a.shape, lambda bb, s: (0,) * a.ndim)
    return pl.pallas_call(
        functools.partial(_out_ln_kernel, alpha=alpha),
        grid=(bsz, seq // tm),
        in_specs=[pl.BlockSpec((1, tm, d), lambda bb, s: (bb, s, 0)),
                  pl.BlockSpec((1, ya.shape[1], tm, LANE), lambda bb, s: (bb, 0, s, 0)),
                  pl.BlockSpec((1, yb.shape[1], tm, LANE), lambda bb, s: (bb, 0, s, 0)),
                  pl.BlockSpec((ka, d), lambda bb, s: (0, 0)),
                  pl.BlockSpec((kb, d), lambda bb, s: (ka // kb, 0)),
                  full(g), full(b)],
        out_specs=pl.BlockSpec((1, tm, d), lambda bb, s: (bb, s, 0)),
        out_shape=jax.ShapeDtypeStruct((bsz, seq, d), F32),
        compiler_params=_cparams(("arbitrary", "arbitrary")),
        name="out_ln",
    )(x, ya, yb, w, w, g, b)


def _mlp_ln_kernel(x_ref, w1_ref, w2_ref, g_ref, b_ref, o_ref, xb_ref, *, alpha):
    f = pl.program_id(1)

    @pl.when(f == 0)
    def _():
        xb_ref[...] = x_ref[...].astype(BF16)
        o_ref[...] = jnp.zeros_like(o_ref)

    hid = jnp.maximum(_dot(xb_ref[...], w1_ref[...]), 0.0)
    o_ref[...] += _dot((hid * hid).astype(BF16), w2_ref[...])

    @pl.when(f == pl.num_programs(1) - 1)
    def _():
        o_ref[...] = _layer_norm(alpha * x_ref[...] + o_ref[...], g_ref[...], b_ref[...])


def _mlp_ln(x, w1, w2, g, b, *, alpha, tm, tf):
    m, d = x.shape
    ff = w1.shape[1]
    return pl.pallas_call(
        functools.partial(_mlp_ln_kernel, alpha=alpha),
        grid=(m // tm, ff // tf),
        in_specs=[pl.BlockSpec((tm, d), lambda i, f: (i, 0)),
                  pl.BlockSpec((d, tf), lambda i, f: (0, f)),
                  pl.BlockSpec((tf, d), lambda i, f: (f, 0)),
                  pl.BlockSpec((1, d), lambda i, f: (0, 0)),
                  pl.BlockSpec((1, d), lambda i, f: (0, 0))],
        out_specs=pl.BlockSpec((tm, d), lambda i, f: (i, 0)),
        out_shape=jax.ShapeDtypeStruct((m, d), F32),
        scratch_shapes=[pltpu.VMEM((tm, d), BF16)],
        compiler_params=_cparams(("arbitrary", "arbitrary")),
        name="mlp_ln",
    )(x, w1, w2, g, b)


CAST_BLOCK_BYTES = 8 * 1024 * 1024


def _cast_kernel(w_ref, o_ref):
    o_ref[...] = w_ref[0].astype(o_ref.dtype)


def _layer_to_bf16(w, layer):
    _, rows, cols = w.shape
    tr = max(8, min(rows, CAST_BLOCK_BYTES // (4 * cols)))
    assert rows % tr == 0
    return pl.pallas_call(
        _cast_kernel,
        grid=(rows // tr,),
        in_specs=[pl.BlockSpec((1, tr, cols), lambda i: (layer, i, 0))],
        out_specs=pl.BlockSpec((tr, cols), lambda i: (i, 0)),
        out_shape=jax.ShapeDtypeStruct((rows, cols), BF16),
        compiler_params=_cparams(("arbitrary",)),
        name="cast_bf16",
    )(w)


def _even_w_in_layout(w):
    qa = w[:, :A_Q_W]
    ka = w[:, A_Q_W:A_Q_W + A_KV_W]
    va = w[:, A_Q_W + A_KV_W:A_Q_W + 2 * A_KV_W]
    hb = w[:, A_Q_W + 2 * A_KV_W:]
    dup = lambda kv: [kv[:, h * HEAD_DIM:(h + 1) * HEAD_DIM] for h in range(A_KV_HEADS) for _ in range(2)]
    return jnp.concatenate([qa, hb] + dup(ka) + dup(va), axis=1).astype(BF16)


def _rope_tables(seq):
    inv_freq = ROPE_BASE ** (-jnp.arange(0, D_ROPE, 2, dtype=F32) / D_ROPE)
    ang = jnp.arange(seq, dtype=F32)[:, None] * inv_freq[None, :]
    cos, sin = jnp.cos(ang), jnp.sin(ang)
    reps = LANE // D_ROPE
    cos_t = jnp.tile(jnp.concatenate([cos, cos], axis=1), (1, reps))
    sin_t = jnp.tile(jnp.concatenate([-sin, sin], axis=1), (1, reps))
    return cos_t, sin_t


def _even_layer(x, w_in, sinks, w_out, g, b, alpha):
    h = _proj(x, _even_w_in_layout(w_in), tm=512, n_split=2)
    n_qa = A_Q_W // LANE
    n_b = 3 * len(B_PATTERNS) * B_W // LANE
    oa = _attn_a(h, sinks, q_tile0=0, k_tile0=n_qa + n_b, v_tile0=n_qa + n_b + A_KV_HEADS)
    ob = _attn_b(h, tile0=n_qa)
    return _out_ln(x, oa, ob, w_out.astype(BF16), g, b, alpha=alpha, tm=512)


def _odd_layer(x, w_in, gq, gkv, w_uq, w_ukv, w_out, g, b, alpha):
    seq = x.shape[1]
    hc = _proj(x, w_in[:, :3 * C_W].astype(BF16), tm=512, n_split=1)
    oc = _attn_c(hc, t=256, group=8)
    w_lat = w_in[:, 3 * C_W:]
    kr_w = w_lat[:, D_Q_RANK + D_KV_RANK:]
    w_lat = jnp.concatenate([w_lat[:, :D_Q_RANK + D_KV_RANK]] + [kr_w] * (LANE // D_ROPE), axis=1).astype(BF16)
    uq = w_uq.reshape(D_Q_RANK, D_HEADS, D_NOPE + D_ROPE)
    uq = jnp.concatenate([uq[:, :, :D_NOPE].reshape(D_Q_RANK, -1), uq[:, :, D_NOPE:].reshape(D_Q_RANK, -1)], axis=1)
    ukv = w_ukv.reshape(D_KV_RANK, D_HEADS, D_NOPE + D_V)
    ukv = jnp.concatenate([ukv[:, :, :D_NOPE].reshape(D_KV_RANK, -1), ukv[:, :, D_NOPE:].reshape(D_KV_RANK, -1)], axis=1)
    cos_t, sin_t = _rope_tables(seq)
    qn, qr, kn, vd, kr = _mla_proj(x, w_lat, gq.reshape(1, -1), gkv.reshape(1, -1),
                                   uq.astype(BF16), ukv.astype(BF16), cos_t, sin_t, tm=512)
    od = _attn_d(qn, qr, kn, kr, vd, t=256, group=8)
    return _out_ln(x, oc, od, w_out.astype(BF16), g, b, alpha=alpha, tm=512)


def kernel(x, even_w_in, even_sinks, even_w_out, odd_w_in, odd_q_norm_g, odd_kv_norm_g, odd_w_uq, odd_w_ukv,
           odd_w_out, ln1_g, ln1_b, mlp_w1, mlp_w2, ln2_g, ln2_b):
    bsz, seq, d = x.shape
    depth = ln1_g.shape[0]
    alpha = (2 * depth) ** 0.25
    for layer in range(depth):
        j = layer // 2
        g1, b1 = ln1_g[layer].reshape(1, d), ln1_b[layer].reshape(1, d)
        if layer % 2 == 0:
            x = _even_layer(x, even_w_in[j], even_sinks[j], even_w_out[j], g1, b1, alpha)
        else:
            x = _odd_layer(x, odd_w_in[j], odd_q_norm_g[j], odd_kv_norm_g[j], odd_w_uq[j], odd_w_ukv[j],
                           odd_w_out[j], g1, b1, alpha)
        x = _mlp_ln(x.reshape(bsz * seq, d), _layer_to_bf16(mlp_w1, layer), _layer_to_bf16(mlp_w2, layer),
                    ln2_g[layer].reshape(1, d), ln2_b[layer].reshape(1, d), alpha=alpha, tm=1024, tf=512
                    ).reshape(bsz, seq, d)
    return x
```

```python
import functools
import math

import jax
import jax.numpy as jnp
from jax import lax
from jax.experimental import pallas as pl
from jax.experimental.pallas import tpu as pltpu

LANE = 128
V7X_VMEM_BYTES = 64 * 1024 * 1024
VMEM_LIMIT = V7X_VMEM_BYTES - 8 * 1024 * 1024

HEAD_DIM = 64
A_Q_HEADS = 16
A_KV_HEADS = 2
A_WINDOW = 128
B_HEADS = 8
B_PATTERNS = ((128, 1), (512, 4), (2048, 16))
C_HEADS = 16
D_HEADS = 16
D_Q_RANK = 512
D_KV_RANK = 256
D_NOPE = 64
D_ROPE = 32
D_V = 64
ROPE_BASE = 10000.0
LN_EPS = 1e-5
RMS_EPS = 1e-6
BLOCK = 128
NEG = -1e30

A_Q_W = A_Q_HEADS * HEAD_DIM
A_KV_W = A_KV_HEADS * HEAD_DIM
B_W = B_HEADS * HEAD_DIM
C_W = C_HEADS * HEAD_DIM

BF16 = jnp.bfloat16
F32 = jnp.float32

PROJ_ROWS = 512
OUT_LN_ROWS = 512
OUT_LN_CHUNKS = 2
MLP_ROWS = 1024
MLP_HIDDEN = 512
CAUSAL_TILE = 256
CAUSAL_CHAINS = 8
CAST_BLOCK_BYTES = 8 * 1024 * 1024


def _cparams(sem):
    return pltpu.CompilerParams(dimension_semantics=sem, vmem_limit_bytes=VMEM_LIMIT)


def _dot(a, b):
    return jnp.dot(a, b, preferred_element_type=F32)


def _dot_nt(a, b):
    return lax.dot_general(a, b, (((1,), (1,)), ((), ())), preferred_element_type=F32)


def _layer_norm(r, g, b):
    mu = jnp.mean(r, axis=-1, keepdims=True)
    rc = r - mu
    var = jnp.mean(rc * rc, axis=-1, keepdims=True)
    return rc * lax.rsqrt(var + LN_EPS) * g + b


def _proj_kernel(x_ref, w_ref, o_ref):
    xb = x_ref[0].astype(BF16)
    acc = _dot(xb, w_ref[...])
    for c in range(o_ref.shape[1]):
        o_ref[0, c] = acc[:, c * LANE:(c + 1) * LANE].astype(o_ref.dtype)


def _proj(x, w, *, tm, n_split):
    bsz, seq, d = x.shape
    tiles = w.shape[1] // LANE
    tps = tiles // n_split
    return pl.pallas_call(
        _proj_kernel,
        grid=(n_split, bsz, seq // tm),
        in_specs=[pl.BlockSpec((1, tm, d), lambda n, b, s: (b, s, 0)),
                  pl.BlockSpec((d, tps * LANE), lambda n, b, s: (0, n))],
        out_specs=pl.BlockSpec((1, tps, tm, LANE), lambda n, b, s: (b, n, s, 0)),
        out_shape=jax.ShapeDtypeStruct((bsz, tiles, seq, LANE), BF16),
        compiler_params=_cparams(("arbitrary", "arbitrary", "arbitrary")),
        name="proj",
    )(x, w)


def _split_heads(q2, lo):
    zero = jnp.zeros_like(q2)
    return jnp.concatenate([jnp.where(lo, q2, zero), jnp.where(lo, zero, q2)], axis=0)


def _merge_heads(o, lo, t):
    return jnp.where(lo, o[:t], o[t:])


def _lanes(x, n):
    return x[:, :n] if n <= LANE else jnp.concatenate([x] * (n // LANE), axis=1)


def _row_stat(x):
    return jnp.broadcast_to(x, (x.shape[0], LANE))


def _fold_lanes(p):
    out = p[:, :LANE]
    for c in range(1, p.shape[1] // LANE):
        out = out + p[:, c * LANE:(c + 1) * LANE]
    return out


def _finish(acc, l_part):
    return acc / _row_stat(jnp.sum(l_part, axis=-1, keepdims=True))


def _skewed(n_chains, stages):
    state = [dict() for _ in range(n_chains)]
    for step in range(n_chains + len(stages) - 1):
        for g in range(n_chains):
            s = step - g
            if 0 <= s < len(stages):
                stages[s](g, state[g])


def _rel(t, n):
    row = lax.broadcasted_iota(jnp.int32, (2 * t, n), 0)
    col = lax.broadcasted_iota(jnp.int32, (2 * t, n), 1)
    return jnp.where(row >= t, row - t, row) - col


def _per_head(t, n, even, odd):
    row = lax.broadcasted_iota(jnp.int32, (2 * t, n), 0)
    return jnp.where(row >= t, jnp.asarray(odd, F32), jnp.asarray(even, F32))


def _online_softmax_steps(ss, vs, m_ref, l_ref, acc_ref, exp=jnp.exp):
    def scores(g, st):
        st["s"] = ss[g]()

    def probabilities(g, st):
        s = st.pop("s")
        m_prev = m_ref[g]
        m_new = jnp.maximum(m_prev, _row_stat(jnp.max(s, axis=-1, keepdims=True)))
        alpha = exp(m_prev - m_new)
        p = exp(s - _lanes(m_new, s.shape[1]))
        l_ref[g] = alpha * l_ref[g] + _fold_lanes(p)
        m_ref[g] = m_new
        st["p"] = p.astype(BF16)
        st["alpha"] = alpha

    def values(g, st):
        acc_ref[g] = st.pop("alpha") * acc_ref[g] + _dot(st.pop("p"), vs[g]())

    state = [dict() for _ in ss]
    for stage in (scores, probabilities, values):
        for g, st in enumerate(state):
            stage(g, st)


def _window_start(i, back, n_blocks, seq_blocks):
    return jnp.clip(i - back, 0, seq_blocks - n_blocks)


def _attn_a_kernel(sinks_ref, q_ref, k_ref, v_ref, o_ref, *, t):
    i = pl.program_id(1)
    n_tiles = q_ref.shape[1]
    seq_blocks = k_ref.shape[2] // t
    win_blocks = min(2, seq_blocks)
    n = win_blocks * t
    heads_per_kv = A_Q_HEADS // A_KV_HEADS
    lo = lax.broadcasted_iota(jnp.int32, (1, LANE), 1) < HEAD_DIM
    start = _window_start(i, 1, win_blocks, seq_blocks)
    rows = pl.ds(pl.multiple_of(start * t, t), n)
    dist = _rel(t, n) + (i - start) * t
    distf = dist.astype(F32)
    valid = (dist >= 0) & (dist < A_WINDOW)
    hi_half = lax.broadcasted_iota(jnp.int32, (2 * t, LANE), 0) >= t
    kv = [(k_ref[0, kh, rows, :], v_ref[0, kh, rows, :]) for kh in range(A_KV_HEADS)]

    def scores(hp, st):
        q = _split_heads(q_ref[0, hp] * jnp.asarray(1.0 / math.sqrt(HEAD_DIM), BF16), lo)
        slope = _per_head(t, n, 2.0 ** (-(2 * hp + 1) * 8.0 / A_Q_HEADS), 2.0 ** (-(2 * hp + 2) * 8.0 / A_Q_HEADS))
        st["s"] = jnp.where(valid, _dot_nt(q, kv[(2 * hp) // heads_per_kv][0]) - slope * distf, NEG)

    def probabilities(hp, st):
        s = st.pop("s")
        sink = jnp.where(hi_half, sinks_ref[2 * hp + 1], sinks_ref[2 * hp])
        m = jnp.maximum(_row_stat(jnp.max(s, axis=-1, keepdims=True)), sink)
        p = jnp.exp(s - _lanes(m, n))
        st["denom"] = _row_stat(jnp.sum(p, axis=-1, keepdims=True)) + jnp.exp(sink - m)
        st["p"] = p.astype(BF16)

    def values(hp, st):
        o = _dot(st.pop("p"), kv[(2 * hp) // heads_per_kv][1]) / st.pop("denom")
        o_ref[0, hp] = _merge_heads(o, lo, t).astype(o_ref.dtype)

    state = [dict() for _ in range(n_tiles)]
    for stage in (scores, probabilities, values):
        for hp in range(n_tiles):
            stage(hp, state[hp])


def _attn_a(h, sinks, *, q_tile0, k_tile0, v_tile0):
    bsz, _, seq, _ = h.shape
    t = A_WINDOW
    nq = A_Q_HEADS // 2
    return pl.pallas_call(
        functools.partial(_attn_a_kernel, t=t),
        grid=(bsz, seq // t),
        in_specs=[pl.BlockSpec(memory_space=pltpu.SMEM),
                  pl.BlockSpec((1, nq, t, LANE), lambda b, i: (b, q_tile0 // nq, i, 0)),
                  pl.BlockSpec((1, A_KV_HEADS, seq, LANE), lambda b, i: (b, k_tile0 // A_KV_HEADS, 0, 0)),
                  pl.BlockSpec((1, A_KV_HEADS, seq, LANE), lambda b, i: (b, v_tile0 // A_KV_HEADS, 0, 0))],
        out_specs=pl.BlockSpec((1, nq, t, LANE), lambda b, i: (b, 0, i, 0)),
        out_shape=jax.ShapeDtypeStruct((bsz, nq, seq, LANE), BF16),
        compiler_params=_cparams(("arbitrary", "arbitrary")),
        name="attn_a",
    )(sinks, h, h, h)


def _b_slopes(hp):
    return 2.0 ** (-(2 * hp + 1) * 8.0 / B_HEADS), 2.0 ** (-(2 * hp + 2) * 8.0 / B_HEADS)


def _attn_b_dilated_kernel(q_ref, k_ref, v_ref, o_ref, lse_ref, qf_ref, kf_ref, vf_ref, *, window, dil):
    hp = pl.program_id(1)
    seq = q_ref.shape[2]
    n = seq // dil
    lo = lax.broadcasted_iota(jnp.int32, (1, LANE), 1) < HEAD_DIM
    qf_ref[...] = q_ref[0, 0].astype(F32)
    kf_ref[...] = k_ref[0, 0].astype(F32)
    vf_ref[...] = v_ref[0, 0].astype(F32)
    tb = min(n, BLOCK)
    n_blocks = n // tb
    reach = window // dil
    win_blocks = min(reach // tb + 1, n_blocks)
    nk = win_blocks * tb
    head = (2 * hp + (lax.broadcasted_iota(jnp.int32, (2 * tb, nk), 0) >= tb).astype(jnp.int32)).astype(F32)
    slope = jnp.exp2(-(head + 1.0) * (8.0 / B_HEADS))
    qscale = jnp.asarray(1.0 / math.sqrt(HEAD_DIM), BF16)
    biases = {}
    chains = []
    for j in range(n_blocks):
        start = min(max(j - reach // tb, 0), n_blocks - win_blocks)
        if j - start not in biases:
            dist = _rel(tb, nk) + (j - start) * tb
            biases[j - start] = jnp.where((dist >= 0) & (dist <= reach), -slope * (dist * dil).astype(F32), NEG)
        for r in range(dil):
            chains.append((pl.ds(r + dil * j * tb, tb, stride=dil), pl.ds(r + dil * start * tb, nk, stride=dil),
                           biases[j - start]))

    def scores(c, st):
        q_rows, k_rows, bias = chains[c]
        q = _split_heads(qf_ref[q_rows, :].astype(BF16) * qscale, lo)
        st["s"] = _dot_nt(q, kf_ref[k_rows, :].astype(BF16)) + bias

    def probabilities(c, st):
        s = st.pop("s")
        m = _row_stat(jnp.max(s, axis=-1, keepdims=True))
        p = jnp.exp(s - _lanes(m, nk))
        st["l"] = _row_stat(jnp.sum(p, axis=-1, keepdims=True))
        st["m"] = m
        st["p"] = p.astype(BF16)

    def values(c, st):
        q_rows, k_rows, _ = chains[c]
        l = st.pop("l")
        o = _dot(st.pop("p"), vf_ref[k_rows, :].astype(BF16)) / l
        o_ref[0, 0, q_rows, :] = _merge_heads(o, lo, tb)
        lse_ref[0, 0, q_rows, :] = _merge_heads(st.pop("m") + jnp.log(l), lo, tb)

    state = [dict() for _ in chains]
    for stage in (scores, probabilities, values):
        for c in range(len(chains)):
            stage(c, state[c])


def _attn_b_dilated(h, *, tile0, window, dil):
    bsz, _, seq, _ = h.shape
    n = B_HEADS // 2
    spec = lambda off: pl.BlockSpec((1, 1, seq, LANE), lambda b, hp: (b, tile0 + off + hp, 0, 0))
    out_spec = pl.BlockSpec((1, 1, seq, LANE), lambda b, hp: (b, hp, 0, 0))
    out_shape = jax.ShapeDtypeStruct((bsz, n, seq, LANE), F32)
    return pl.pallas_call(
        functools.partial(_attn_b_dilated_kernel, window=window, dil=dil),
        grid=(bsz, n),
        in_specs=[spec(0), spec(n), spec(2 * n)],
        out_specs=[out_spec, out_spec],
        out_shape=[out_shape, out_shape],
        scratch_shapes=[pltpu.VMEM((seq, LANE), F32)] * 3,
        compiler_params=_cparams(("arbitrary", "arbitrary")),
        name="attn_b_dilated",
    )(h, h, h)


def _attn_b_kernel(*refs, t, patterns, n_extra):
    groups = [refs[3 * g:3 * g + 3] for g in range(len(patterns))]
    extras = [refs[3 * len(patterns) + 2 * e:3 * len(patterns) + 2 * e + 2] for e in range(n_extra)]
    o_ref = refs[-1]
    i = pl.program_id(1)
    n_tiles = o_ref.shape[1]
    seq_blocks = groups[0][1].shape[2] // t
    lo = lax.broadcasted_iota(jnp.int32, (1, LANE), 1) < HEAD_DIM
    qscale = jnp.asarray(1.0 / math.sqrt(HEAD_DIM), BF16)
    slopes = [_b_slopes(hp) for hp in range(n_tiles)]

    bounded = []
    for (q_ref, k_ref, v_ref), (window, dil) in zip(groups, patterns):
        back = window // t
        win_blocks = min(back + 1, seq_blocks)
        n = win_blocks * t
        start = _window_start(i, back, win_blocks, seq_blocks)
        dist = _rel(t, n) + (i - start) * t
        valid = (dist >= 0) & (dist <= window) & ((dist & (dil - 1)) == 0)
        bounded.append((q_ref, k_ref, v_ref, pl.ds(pl.multiple_of(start * t, t), n), n, dist.astype(F32), valid))

    def scores(hp, st):
        st["s"] = []
        for q_ref, k_ref, v_ref, rows, n, distf, valid in bounded:
            q = _split_heads(q_ref[0, hp] * qscale, lo)
            s = _dot_nt(q, k_ref[0, hp, rows, :]) - _per_head(t, n, *slopes[hp]) * distf
            st["s"].append(jnp.where(valid, s, NEG))

    def probabilities(hp, st):
        m = jnp.full((2 * t, LANE), NEG, F32)
        for s in st["s"]:
            m = jnp.maximum(m, _row_stat(jnp.max(s, axis=-1, keepdims=True)))
        l = jnp.zeros((2 * t, LANE), F32)
        st["p"] = []
        for s in st.pop("s"):
            p = jnp.exp(s - _lanes(m, s.shape[1]))
            l = l + _fold_lanes(p)
            st["p"].append(p.astype(BF16))
        st["m"] = m
        st["l"] = _row_stat(jnp.sum(l, axis=-1, keepdims=True))

    def values(hp, st):
        acc = jnp.zeros((2 * t, LANE), F32)
        for p, (_, _, v_ref, rows, _, _, _) in zip(st.pop("p"), bounded):
            acc = acc + _dot(p, v_ref[0, hp, rows, :])
        m, l, acc = (_merge_heads(x, lo, t) for x in (st.pop("m"), st.pop("l"), acc))
        top = m
        for _, lse_ref in extras:
            top = jnp.maximum(top, lse_ref[0, hp])
        scale = jnp.exp(m - top)
        numer, denom = scale * acc, scale * l
        for oe_ref, lse_ref in extras:
            weight = jnp.exp(lse_ref[0, hp] - top)
            numer = numer + weight * oe_ref[0, hp]
            denom = denom + weight
        o_ref[0, hp] = (numer / denom).astype(o_ref.dtype)

    state = [dict() for _ in range(n_tiles)]
    for stage in (scores, probabilities, values):
        for hp in range(n_tiles):
            stage(hp, state[hp])


def _attn_b(h, *, tile0):
    bsz, _, seq, _ = h.shape
    t = BLOCK
    n = B_HEADS // 2
    in_specs, bounded, extras = [], [], []
    for g, (window, dil) in enumerate(B_PATTERNS):
        base = tile0 + 3 * n * g
        if dil > 1:
            extras.extend(_attn_b_dilated(h, tile0=base, window=window, dil=dil))
            continue
        bounded.append((window, dil))
        in_specs.append(pl.BlockSpec((1, n, t, LANE), lambda b, i, base=base: (b, base // n, i, 0)))
        in_specs.append(pl.BlockSpec((1, n, seq, LANE), lambda b, i, base=base: (b, base // n + 1, 0, 0)))
        in_specs.append(pl.BlockSpec((1, n, seq, LANE), lambda b, i, base=base: (b, base // n + 2, 0, 0)))
    assert bounded, "at least one undilated group is expected"
    in_specs += [pl.BlockSpec((1, n, t, LANE), lambda b, i: (b, 0, i, 0))] * len(extras)
    return pl.pallas_call(
        functools.partial(_attn_b_kernel, t=t, patterns=tuple(bounded), n_extra=len(extras) // 2),
        grid=(bsz, seq // t),
        in_specs=in_specs,
        out_specs=pl.BlockSpec((1, n, t, LANE), lambda b, i: (b, 0, i, 0)),
        out_shape=jax.ShapeDtypeStruct((bsz, n, seq, LANE), BF16),
        compiler_params=_cparams(("arbitrary", "arbitrary")),
        name="attn_b",
    )(*([h] * (3 * len(bounded)) + extras))


def _attn_c_kernel(q_ref, k_ref, v_ref, o_ref, c_ref, acc_ref, *, t, group):
    i = pl.program_id(1)
    n_tiles = q_ref.shape[1]
    lo = lax.broadcasted_iota(jnp.int32, (1, LANE), 1) < HEAD_DIM
    strict = _rel(t, t) > 0
    jj = lax.broadcasted_iota(jnp.int32, (t, t), 0)
    ss = lax.broadcasted_iota(jnp.int32, (t, t), 1)
    tri = jnp.where(jj >= ss, 1.0, 0.0).astype(BF16)
    qscale = jnp.asarray(1.0 / math.sqrt(HEAD_DIM), BF16)

    def per_group(gi, carry):
        hps = [gi * group + g for g in range(group)]
        qs = [_split_heads(q_ref[0, hp] * qscale, lo) for hp in hps]
        for g in range(group):
            c_ref[g] = jnp.zeros((2 * t, LANE), F32)
            acc_ref[g] = jnp.zeros((2 * t, LANE), F32)

        def step(kb, diagonal):
            rows = pl.ds(pl.multiple_of(kb * t, t), t)

            def scores(g, st):
                st["z"] = _dot_nt(qs[g], k_ref[0, hps[g], rows, :])

            def logs(g, st):
                z = st["z"]
                e = jnp.exp2(jnp.abs(z) * -math.log2(math.e))
                drop = jnp.maximum(z, 0.0) + jnp.log(1.0 + e)
                if diagonal:
                    drop = jnp.where(strict, drop, 0.0)
                st["drop"] = drop.astype(BF16)
                st["row_sum"] = _row_stat(jnp.sum(drop, axis=-1, keepdims=True))

            def suffix(g, st):
                st["drops"] = _dot(st.pop("drop"), tri)

            def weights(g, st):
                w = jnp.exp(st.pop("z") - st.pop("drops") - _lanes(c_ref[g], t))
                if diagonal:
                    w = jnp.where(strict, w, 0.0)
                st["w"] = w.astype(BF16)
                c_ref[g] += st.pop("row_sum")

            def values(g, st):
                acc_ref[g] += _dot(st.pop("w"), v_ref[0, hps[g], rows, :])

            _skewed(group, [scores, logs, suffix, weights, values])

        step(i, True)

        def body(j, c):
            step(i - 1 - j, False)
            return c

        lax.fori_loop(0, i, body, 0)
        for g, hp in enumerate(hps):
            o_ref[0, hp] = _merge_heads(acc_ref[g], lo, t).astype(o_ref.dtype)
        return carry

    lax.fori_loop(0, n_tiles // group, per_group, 0)


def _attn_c(h, *, t, group):
    bsz, _, seq, _ = h.shape
    n = C_HEADS // 2
    return pl.pallas_call(
        functools.partial(_attn_c_kernel, t=t, group=group),
        grid=(bsz, seq // t),
        in_specs=[pl.BlockSpec((1, n, t, LANE), lambda b, i: (b, 0, i, 0)),
                  pl.BlockSpec((1, n, seq, LANE), lambda b, i: (b, 1, 0, 0)),
                  pl.BlockSpec((1, n, seq, LANE), lambda b, i: (b, 2, 0, 0))],
        out_specs=pl.BlockSpec((1, n, t, LANE), lambda b, i: (b, 0, i, 0)),
        out_shape=jax.ShapeDtypeStruct((bsz, n, seq, LANE), BF16),
        scratch_shapes=[pltpu.VMEM((group, 2 * t, LANE), F32)] * 2,
        compiler_params=_cparams(("arbitrary", "arbitrary")),
        name="attn_c",
    )(h, h, h)


def _rope(x, cos_t, sin_t, first_half):
    partner = jnp.where(first_half, pltpu.roll(x, LANE - D_ROPE // 2, 1), pltpu.roll(x, D_ROPE // 2, 1))
    return x * cos_t + partner * sin_t


def _mla_proj_kernel(x_ref, wl_ref, gq_ref, gkv_ref, wuq_ref, wukv_ref, cos_ref, sin_ref,
                     qn_ref, qr_ref, kn_ref, vd_ref, kr_ref):
    xb = x_ref[0].astype(BF16)
    lat = _dot(xb, wl_ref[...])
    cq = lat[:, :D_Q_RANK]
    ckv = lat[:, D_Q_RANK:D_Q_RANK + D_KV_RANK]
    kr = lat[:, D_Q_RANK + D_KV_RANK:]
    cqn = cq * lax.rsqrt(jnp.mean(cq * cq, axis=-1, keepdims=True) + RMS_EPS) * gq_ref[...]
    ckvn = ckv * lax.rsqrt(jnp.mean(ckv * ckv, axis=-1, keepdims=True) + RMS_EPS) * gkv_ref[...]
    q = _dot(cqn.astype(BF16), wuq_ref[...])
    kv = _dot(ckvn.astype(BF16), wukv_ref[...])
    cos_t = cos_ref[...]
    sin_t = sin_ref[...]
    first_half = (lax.broadcasted_iota(jnp.int32, (1, LANE), 1) % D_ROPE) < D_ROPE // 2
    n_nope = qn_ref.shape[1]
    for c in range(n_nope):
        qn_ref[0, c] = q[:, c * LANE:(c + 1) * LANE].astype(BF16)
        kn_ref[0, c] = kv[:, c * LANE:(c + 1) * LANE].astype(BF16)
        vd_ref[0, c] = kv[:, (n_nope + c) * LANE:(n_nope + c + 1) * LANE].astype(BF16)
    for c in range(qr_ref.shape[1]):
        qr = q[:, (n_nope + c) * LANE:(n_nope + c + 1) * LANE]
        qr_ref[0, c] = _rope(qr, cos_t, sin_t, first_half).astype(BF16)
    kr_ref[0] = _rope(kr, cos_t, sin_t, first_half).astype(BF16)


def _mla_proj(x, w_lat, gq, gkv, w_uq, w_ukv, cos_t, sin_t, *, tm):
    bsz, seq, d = x.shape
    n_nope = D_HEADS * D_NOPE // LANE
    n_rope = D_HEADS * D_ROPE // LANE
    full = lambda a: pl.BlockSpec(a.shape, lambda b, s: (0,) * a.ndim)
    tile_out = lambda n: pl.BlockSpec((1, n, tm, LANE), lambda b, s: (b, 0, s, 0))
    tile_shape = lambda n: jax.ShapeDtypeStruct((bsz, n, seq, LANE), BF16)
    return pl.pallas_call(
        _mla_proj_kernel,
        grid=(bsz, seq // tm),
        in_specs=[pl.BlockSpec((1, tm, d), lambda b, s: (b, s, 0)),
                  full(w_lat), full(gq), full(gkv), full(w_uq), full(w_ukv),
                  pl.BlockSpec((tm, LANE), lambda b, s: (s, 0)),
                  pl.BlockSpec((tm, LANE), lambda b, s: (s, 0))],
        out_specs=[tile_out(n_nope), tile_out(n_rope), tile_out(n_nope), tile_out(n_nope),
                   pl.BlockSpec((1, tm, LANE), lambda b, s: (b, s, 0))],
        out_shape=[tile_shape(n_nope), tile_shape(n_rope), tile_shape(n_nope), tile_shape(n_nope),
                   jax.ShapeDtypeStruct((bsz, seq, LANE), BF16)],
        compiler_params=_cparams(("arbitrary", "arbitrary")),
        name="mla_proj",
    )(x, w_lat, gq, gkv, w_uq, w_ukv, cos_t, sin_t)


def _attn_d_kernel(qn_ref, qr_ref, kn_ref, kr_ref, vd_ref, o_ref, m_ref, l_ref, acc_ref, *, t, group):
    i = pl.program_id(1)
    n_tiles = qn_ref.shape[1]
    lane = lax.broadcasted_iota(jnp.int32, (1, LANE), 1)
    lo = lane < D_NOPE
    lane_q = lane // D_ROPE
    causal = _rel(t, t) >= 0
    scale = math.log2(math.e) / math.sqrt(D_NOPE + D_ROPE)

    def per_group(gi, carry):
        hps = [gi * group + g for g in range(group)]
        qs = []
        for g, hp in enumerate(hps):
            qn = _split_heads(qn_ref[0, hp], lo)
            qr2 = qr_ref[0, gi * (group // 2) + g // 2]
            quarter = (g % 2) * 2
            zero = jnp.zeros_like(qr2)
            qr = jnp.concatenate([jnp.where(lane_q == quarter, qr2, zero),
                                  jnp.where(lane_q == quarter + 1, qr2, zero)], axis=0)
            qs.append(jnp.concatenate([qn, qr], axis=1))
            m_ref[g] = jnp.full((2 * t, LANE), NEG, F32)
            l_ref[g] = jnp.zeros((2 * t, LANE), F32)
            acc_ref[g] = jnp.zeros((2 * t, LANE), F32)

        def step(kb, diagonal):
            rows = pl.ds(pl.multiple_of(kb * t, t), t)
            kr = kr_ref[0, rows, :]
            def score_fn(g):
                def fn():
                    s = _dot_nt(qs[g], jnp.concatenate([kn_ref[0, hps[g], rows, :], kr], axis=1)) * scale
                    return jnp.where(causal, s, NEG) if diagonal else s
                return fn

            _online_softmax_steps([score_fn(g) for g in range(group)],
                                  [functools.partial(lambda g: vd_ref[0, hps[g], rows, :], g) for g in range(group)],
                                  m_ref, l_ref, acc_ref, exp=jnp.exp2)

        step(i, True)

        def body(kb, c):
            step(kb, False)
            return c

        lax.fori_loop(0, i, body, 0)
        for g, hp in enumerate(hps):
            o_ref[0, hp] = _merge_heads(_finish(acc_ref[g], l_ref[g]), lo, t).astype(o_ref.dtype)
        return carry

    lax.fori_loop(0, n_tiles // group, per_group, 0)


def _attn_d(qn, qr, kn, kr, vd, *, t, group):
    bsz, n, seq, _ = qn.shape
    nr = qr.shape[1]
    return pl.pallas_call(
        functools.partial(_attn_d_kernel, t=t, group=group),
        grid=(bsz, seq // t),
        in_specs=[pl.BlockSpec((1, n, t, LANE), lambda b, i: (b, 0, i, 0)),
                  pl.BlockSpec((1, nr, t, LANE), lambda b, i: (b, 0, i, 0)),
                  pl.BlockSpec((1, n, seq, LANE), lambda b, i: (b, 0, 0, 0)),
                  pl.BlockSpec((1, seq, LANE), lambda b, i: (b, 0, 0)),
                  pl.BlockSpec((1, n, seq, LANE), lambda b, i: (b, 0, 0, 0))],
        out_specs=pl.BlockSpec((1, n, t, LANE), lambda b, i: (b, 0, i, 0)),
        out_shape=jax.ShapeDtypeStruct((bsz, n, seq, LANE), BF16),
        scratch_shapes=[pltpu.VMEM((group, 2 * t, LANE), F32)] * 3,
        compiler_params=_cparams(("arbitrary", "arbitrary")),
        name="attn_d",
    )(qn, qr, kn, kr, vd)


def _out_ln_kernel(x_ref, ya_ref, yb_ref, wa_ref, wb_ref, g_ref, b_ref, o_ref, *, alpha):
    rows_per_chunk = x_ref.shape[1] // OUT_LN_CHUNKS
    chunks = [pl.ds(c * rows_per_chunk, rows_per_chunk) for c in range(OUT_LN_CHUNKS)]
    mixed = []
    for rows in chunks:
        ya = jnp.concatenate([ya_ref[0, j, rows, :] for j in range(ya_ref.shape[1])], axis=1)
        yb = jnp.concatenate([yb_ref[0, j, rows, :] for j in range(yb_ref.shape[1])], axis=1)
        mixed.append(_dot(ya, wa_ref[...]) + _dot(yb, wb_ref[...]))
    for rows, mix in zip(chunks, mixed):
        o_ref[0, rows, :] = _layer_norm(alpha * x_ref[0, rows, :] + mix, g_ref[...], b_ref[...])


def _out_ln(x, ya, yb, w, g, b, *, alpha, tm):
    bsz, seq, d = x.shape
    ka, kb = ya.shape[1] * LANE, yb.shape[1] * LANE
    assert w.shape[0] == ka + kb and ka % kb == 0
    full = lambda a: pl.BlockSpec(a.shape, lambda bb, s: (0,) * a.ndim)
    return pl.pallas_call(
        functools.partial(_out_ln_kernel, alpha=alpha),
        grid=(bsz, seq // tm),
        in_specs=[pl.BlockSpec((1, tm, d), lambda bb, s: (bb, s, 0)),
                  pl.BlockSpec((1, ya.shape[1], tm, LANE), lambda bb, s: (bb, 0, s, 0)),
                  pl.BlockSpec((1, yb.shape[1], tm, LANE), lambda bb, s: (bb, 0, s, 0)),
                  pl.BlockSpec((ka, d), lambda bb, s: (0, 0)),
                  pl.BlockSpec((kb, d), lambda bb, s: (ka // kb, 0)),
                  full(g), full(b)],
        out_specs=pl.BlockSpec((1, tm, d), lambda bb, s: (bb, s, 0)),
        out_shape=jax.ShapeDtypeStruct((bsz, seq, d), F32),
        compiler_params=_cparams(("arbitrary", "arbitrary")),
        name="out_ln",
    )(x, ya, yb, w, w, g, b)


def _mlp_ln_kernel(x_ref, w1_ref, w2_ref, g_ref, b_ref, o_ref, xb_ref, *, alpha):
    f = pl.program_id(1)

    @pl.when(f == 0)
    def _():
        xb_ref[...] = x_ref[...].astype(BF16)
        o_ref[...] = jnp.zeros_like(o_ref)

    hid = jnp.maximum(_dot(xb_ref[...], w1_ref[...]), 0.0)
    o_ref[...] += _dot((hid * hid).astype(BF16), w2_ref[...])

    @pl.when(f == pl.num_programs(1) - 1)
    def _():
        o_ref[...] = _layer_norm(alpha * x_ref[...] + o_ref[...], g_ref[...], b_ref[...])


def _mlp_ln(x, w1, w2, g, b, *, alpha, tm, tf):
    m, d = x.shape
    ff = w1.shape[1]
    return pl.pallas_call(
        functools.partial(_mlp_ln_kernel, alpha=alpha),
        grid=(m // tm, ff // tf),
        in_specs=[pl.BlockSpec((tm, d), lambda i, f: (i, 0)),
                  pl.BlockSpec((d, tf), lambda i, f: (0, f)),
                  pl.BlockSpec((tf, d), lambda i, f: (f, 0)),
                  pl.BlockSpec((1, d), lambda i, f: (0, 0)),
                  pl.BlockSpec((1, d), lambda i, f: (0, 0))],
        out_specs=pl.BlockSpec((tm, d), lambda i, f: (i, 0)),
        out_shape=jax.ShapeDtypeStruct((m, d), F32),
        scratch_shapes=[pltpu.VMEM((tm, d), BF16)],
        compiler_params=_cparams(("arbitrary", "arbitrary")),
        name="mlp_ln",
    )(x, w1, w2, g, b)


def _cast_kernel(w_ref, o_ref):
    o_ref[...] = w_ref[0].astype(o_ref.dtype)


def _layer_to_bf16(w, layer):
    _, rows, cols = w.shape
    tr = max(8, min(rows, CAST_BLOCK_BYTES // (4 * cols)))
    assert rows % tr == 0
    return pl.pallas_call(
        _cast_kernel,
        grid=(rows // tr,),
        in_specs=[pl.BlockSpec((1, tr, cols), lambda i: (layer, i, 0))],
        out_specs=pl.BlockSpec((tr, cols), lambda i: (i, 0)),
        out_shape=jax.ShapeDtypeStruct((rows, cols), BF16),
        compiler_params=_cparams(("arbitrary",)),
        name="cast_bf16",
    )(w)


def _even_w_in_layout(w):
    qa = w[:, :A_Q_W]
    ka = w[:, A_Q_W:A_Q_W + A_KV_W]
    va = w[:, A_Q_W + A_KV_W:A_Q_W + 2 * A_KV_W]
    hb = w[:, A_Q_W + 2 * A_KV_W:]
    dup = lambda kv: [kv[:, h * HEAD_DIM:(h + 1) * HEAD_DIM] for h in range(A_KV_HEADS) for _ in range(2)]
    return jnp.concatenate([qa, hb] + dup(ka) + dup(va), axis=1).astype(BF16)


def _rope_tables(seq):
    inv_freq = ROPE_BASE ** (-jnp.arange(0, D_ROPE, 2, dtype=F32) / D_ROPE)
    ang = jnp.arange(seq, dtype=F32)[:, None] * inv_freq[None, :]
    cos, sin = jnp.cos(ang), jnp.sin(ang)
    reps = LANE // D_ROPE
    cos_t = jnp.tile(jnp.concatenate([cos, cos], axis=1), (1, reps))
    sin_t = jnp.tile(jnp.concatenate([-sin, sin], axis=1), (1, reps))
    return cos_t, sin_t


def _even_layer(x, w_in, sinks, w_out, g, b, alpha):
    h = _proj(x, _even_w_in_layout(w_in), tm=PROJ_ROWS, n_split=2)
    n_qa = A_Q_W // LANE
    n_b = 3 * len(B_PATTERNS) * B_W // LANE
    oa = _attn_a(h, sinks, q_tile0=0, k_tile0=n_qa + n_b, v_tile0=n_qa + n_b + A_KV_HEADS)
    ob = _attn_b(h, tile0=n_qa)
    return _out_ln(x, oa, ob, w_out.astype(BF16), g, b, alpha=alpha, tm=OUT_LN_ROWS)


def _odd_layer(x, w_in, gq, gkv, w_uq, w_ukv, w_out, g, b, alpha):
    seq = x.shape[1]
    hc = _proj(x, w_in[:, :3 * C_W].astype(BF16), tm=PROJ_ROWS, n_split=1)
    oc = _attn_c(hc, t=CAUSAL_TILE, group=CAUSAL_CHAINS)
    w_lat = w_in[:, 3 * C_W:]
    kr_w = w_lat[:, D_Q_RANK + D_KV_RANK:]
    w_lat = jnp.concatenate([w_lat[:, :D_Q_RANK + D_KV_RANK]] + [kr_w] * (LANE // D_ROPE), axis=1).astype(BF16)
    uq = w_uq.reshape(D_Q_RANK, D_HEADS, D_NOPE + D_ROPE)
    uq = jnp.concatenate([uq[:, :, :D_NOPE].reshape(D_Q_RANK, -1), uq[:, :, D_NOPE:].reshape(D_Q_RANK, -1)], axis=1)
    ukv = w_ukv.reshape(D_KV_RANK, D_HEADS, D_NOPE + D_V)
    ukv = jnp.concatenate([ukv[:, :, :D_NOPE].reshape(D_KV_RANK, -1), ukv[:, :, D_NOPE:].reshape(D_KV_RANK, -1)], axis=1)
    cos_t, sin_t = _rope_tables(seq)
    qn, qr, kn, vd, kr = _mla_proj(x, w_lat, gq.reshape(1, -1), gkv.reshape(1, -1),
                                   uq.astype(BF16), ukv.astype(BF16), cos_t, sin_t, tm=PROJ_ROWS)
    od = _attn_d(qn, qr, kn, kr, vd, t=CAUSAL_TILE, group=CAUSAL_CHAINS)
    return _out_ln(x, oc, od, w_out.astype(BF16), g, b, alpha=alpha, tm=OUT_LN_ROWS)


def kernel(x, even_w_in, even_sinks, even_w_out, odd_w_in, odd_q_norm_g, odd_kv_norm_g, odd_w_uq, odd_w_ukv,
           odd_w_out, ln1_g, ln1_b, mlp_w1, mlp_w2, ln2_g, ln2_b):
    bsz, seq, d = x.shape
    depth = ln1_g.shape[0]
    alpha = (2 * depth) ** 0.25
    for layer in range(depth):
        j = layer // 2
        g1, b1 = ln1_g[layer].reshape(1, d), ln1_b[layer].reshape(1, d)
        if layer % 2 == 0:
            x = _even_layer(x, even_w_in[j], even_sinks[j], even_w_out[j], g1, b1, alpha)
        else:
            x = _odd_layer(x, odd_w_in[j], odd_q_norm_g[j], odd_kv_norm_g[j], odd_w_uq[j], odd_w_ukv[j],
                           odd_w_out[j], g1, b1, alpha)
        x = _mlp_ln(x.reshape(bsz * seq, d), _layer_to_bf16(mlp_w1, layer), _layer_to_bf16(mlp_w2, layer),
                    ln2_g[layer].reshape(1, d), ln2_b[layer].reshape(1, d), alpha=alpha, tm=MLP_ROWS, tf=MLP_HIDDEN
                    ).reshape(bsz, seq, d)
    return x
```
